```python
import jax, jax.numpy as jnp
from jax import lax
import numpy as np

D_MODEL = 1024
BATCH = 8
SEQ = 4096
DEPTH = 4
DEC_BATCH = 8
DEC_SEQ = 64
PAST_LEN = 2048

CHUNK = 64
D_A = 512
H_A = 4
DH_A = D_A // H_A
D_B = 512
H_B = 4
DH_B = D_B // H_B
D_FF = 2816
CONV_W = 3
ROPE_BASE = 10000.0
NORM_EPS = 1e-6
GN_EPS = 1e-5
F_FLOOR = 1e-30
IN_SIZES = (D_A, D_A, D_A, D_A, D_B, D_B, D_B, D_B, D_MODEL, D_MODEL)
D_IN = 4 * D_A + 4 * D_B + 2 * D_MODEL

kernel_name = "hgrn2_retention_convffn_streaming_step"


def rms_norm(x, w):
    xf = x.astype(jnp.float32)
    y = xf * lax.rsqrt(jnp.mean(xf * xf, axis=-1, keepdims=True) + NORM_EPS)
    return (y * w.astype(jnp.float32)).astype(x.dtype)


def group_norm_heads(x):
    mu = jnp.mean(x, axis=-1, keepdims=True)
    xc = x - mu
    var = jnp.mean(xc * xc, axis=-1, keepdims=True)
    return xc * lax.rsqrt(var + GN_EPS)


def split_heads(x, n):
    b, t, _ = x.shape
    return x.reshape(b, t, n, -1).transpose(0, 2, 1, 3)


def merge_heads(x):
    b, h, t, d = x.shape
    return x.transpose(0, 2, 1, 3).reshape(b, t, h * d)


def rotary(x, pos):
    half = x.shape[-1] // 2
    inv = ROPE_BASE ** (-jnp.arange(half, dtype=jnp.float32) / half)
    ang = pos.astype(jnp.float32)[:, None] * inv[None, :]
    cos, sin = jnp.cos(ang), jnp.sin(ang)
    x1, x2 = x[..., :half], x[..., half:]
    return jnp.concatenate([x1 * cos - x2 * sin, x1 * sin + x2 * cos], axis=-1)


def to_chunks(x, c):
    b, h, t, d = x.shape
    return jnp.moveaxis(x.reshape(b, h, t // c, c, d), 2, 0)


def from_chunks(x):
    n, b, h, c, d = x.shape
    return jnp.moveaxis(x, 0, 2).reshape(b, h, n * c, d)


def hgrn2_chunk_scan(q, k, v, log_f, s0):
    t = q.shape[2]
    c = min(CHUNK, t)
    causal = jnp.tril(jnp.ones((c, c), dtype=bool))[:, :, None]

    def step(s, blk):
        qc, kc, vc, gc = blk
        b = jnp.cumsum(gc, axis=2)
        diff = b[:, :, :, None, :] - b[:, :, None, :, :]
        decay = jnp.where(causal, jnp.exp(jnp.where(causal, diff, 0.0)), 0.0)
        scores = jnp.einsum('bhtk,bhsk,bhtsk->bhts', qc, kc, decay)
        o = (jnp.einsum('bhts,bhsv->bhtv', scores, vc)
             + jnp.einsum('bhtk,bhkv->bhtv', qc * jnp.exp(b), s))
        b_last = b[:, :, -1, :]
        s_new = (jnp.exp(b_last)[..., None] * s
                 + jnp.einsum('bhsk,bhsv->bhkv', kc * jnp.exp(b_last[:, :, None, :] - b), vc))
        return s_new, o

    s_fin, o = lax.scan(step, s0, (to_chunks(q, c), to_chunks(k, c), to_chunks(v, c), to_chunks(log_f, c)))
    return from_chunks(o), s_fin


def retention_chunk_scan(q, k, v, log_gamma, s0):
    t = q.shape[2]
    c = min(CHUNK, t)
    idx = jnp.arange(c, dtype=jnp.float32)
    rel = idx[:, None] - idx[None, :]
    lg = log_gamma[:, None, None]
    dmat = jnp.where(rel >= 0, jnp.exp(lg * jnp.maximum(rel, 0.0)), 0.0)
    q_dec = jnp.exp(log_gamma[:, None] * (idx + 1.0))[:, :, None]
    k_dec = jnp.exp(log_gamma[:, None] * (c - 1.0 - idx))[:, :, None]
    chunk_dec = jnp.exp(log_gamma * c)[:, None, None]

    def step(s, blk):
        qc, kc, vc = blk
        scores = jnp.einsum('bhtd,bhsd->bhts', qc, kc) * dmat
        o = (jnp.einsum('bhts,bhsv->bhtv', scores, vc)
             + jnp.einsum('bhtd,bhdv->bhtv', qc, s) * q_dec)
        s_new = chunk_dec * s + jnp.einsum('bhsd,bhsv->bhdv', kc * k_dec, vc)
        return s_new, o

    s_fin, o = lax.scan(step, s0, (to_chunks(q, c), to_chunks(k, c), to_chunks(v, c)))
    return from_chunks(o), s_fin


def trunk_layer(x, pos, s_hg, s_ret, conv_prev, lb, norm_mix_w, w_in, hg_norm_w, ret_norm_w,
                ret_norm_b, w_branch, w_out, norm_ffn_w, w_up, conv_w, conv_b, w_down):
    dt = x.dtype
    f32 = jnp.float32
    h = rms_norm(x, norm_mix_w)
    proj = jnp.einsum('btd,de->bte', h, w_in)
    splits = [int(v) for v in np.cumsum(IN_SIZES)[:-1]]
    hq, hf, hi, hgate, rq, rk, rv, rgate, gate_a, gate_b = jnp.split(proj, splits, axis=-1)

    q_a = split_heads(jax.nn.silu(hq), H_A).astype(f32)
    z_f = split_heads(hf, H_A).astype(f32)
    lb_h = lb.reshape(1, H_A, 1, DH_A)
    f_gate = lb_h + (1.0 - lb_h) * jax.nn.sigmoid(z_f)
    log_f = jnp.log(jnp.maximum(f_gate, F_FLOOR))
    k_a = 1.0 - f_gate
    v_a = split_heads(hi, H_A).astype(f32)
    o_a, s_hg_new = hgrn2_chunk_scan(q_a, k_a, v_a, log_f, s_hg.astype(f32))
    o_a = rms_norm(o_a, hg_norm_w)
    o_a = merge_heads(o_a).astype(dt) * jax.nn.silu(hgate)

    q_b = rotary(split_heads(rq, H_B).astype(f32), pos)
    k_b = rotary(split_heads(rk, H_B).astype(f32), pos) * (DH_B ** -0.5)
    v_b = split_heads(rv, H_B).astype(f32)
    log_gamma = jnp.log(1.0 - 2.0 ** (-5.0 - jnp.arange(H_B, dtype=f32)))
    o_b, s_ret_new = retention_chunk_scan(q_b, k_b, v_b, log_gamma, s_ret.astype(f32))
    o_b = merge_heads(group_norm_heads(o_b)) * ret_norm_w.astype(f32) + ret_norm_b.astype(f32)
    o_b = o_b.astype(dt) * jax.nn.silu(rgate)

    z_a = jnp.einsum('btc,cd->btd', o_a, w_branch[0])
    z_b = jnp.einsum('btc,cd->btd', o_b, w_branch[1])
    merged = jax.nn.sigmoid(gate_a) * z_a + jax.nn.sigmoid(gate_b) * z_b
    x = x + jnp.einsum('btd,de->bte', merged, w_out)

    h = rms_norm(x, norm_ffn_w)
    up = jnp.einsum('btd,df->btf', h, w_up)
    a, g = jnp.split(up, 2, axis=-1)
    t = a.shape[1]
    ext = jnp.concatenate([conv_prev.astype(dt), a], axis=1)
    conv = conv_b
    for j in range(CONV_W):
        conv = conv + ext[:, j:j + t] * conv_w[j]
    x = x + jnp.einsum('btf,fd->btd', jax.nn.silu(conv) * g, w_down)
    conv_new = ext[:, t:]
    return x, s_hg_new.astype(s_hg.dtype), s_ret_new.astype(s_ret.dtype), conv_new.astype(conv_prev.dtype)


def setup_inputs(seed: int = 0) -> dict:
    key = jax.random.key(seed)
    ks = jax.random.split(key, 20)
    f32 = jnp.float32
    nrm = lambda k, shp, s: jax.random.normal(k, shp, f32) * s
    return {
        "x_prompt": nrm(ks[0], (BATCH, SEQ, D_MODEL), 1.0),
        "x_sample": nrm(ks[1], (DEC_BATCH, DEC_SEQ, D_MODEL), 1.0),
        "state_hgrn": nrm(ks[2], (DEPTH, DEC_BATCH, H_A, DH_A, DH_A), 0.5),
        "state_ret": nrm(ks[3], (DEPTH, DEC_BATCH, H_B, DH_B, DH_B), 1.0),
        "cache_conv": nrm(ks[4], (DEPTH, DEC_BATCH, CONV_W - 1, D_FF), 1.0),
        "norm_mix_w": 1.0 + nrm(ks[5], (DEPTH, D_MODEL), 0.01),
        "w_in": nrm(ks[6], (DEPTH, D_MODEL, D_IN), D_MODEL ** -0.5),
        "hg_lb_logits": nrm(ks[7], (DEPTH, D_A), 0.1),
        "hg_norm_w": 1.0 + nrm(ks[8], (DEPTH, DH_A), 0.01),
        "ret_norm_w": 1.0 + nrm(ks[9], (DEPTH, D_B), 0.01),
        "ret_norm_b": nrm(ks[10], (DEPTH, D_B), 0.01),
        "w_branch": nrm(ks[11], (DEPTH, 2, D_A, D_MODEL), D_A ** -0.5),
        "w_out": nrm(ks[12], (DEPTH, D_MODEL, D_MODEL), D_MODEL ** -0.5),
        "norm_ffn_w": 1.0 + nrm(ks[13], (DEPTH, D_MODEL), 0.01),
        "w_up": nrm(ks[14], (DEPTH, D_MODEL, 2 * D_FF), D_MODEL ** -0.5),
        "conv_w": nrm(ks[15], (DEPTH, CONV_W, D_FF), CONV_W ** -0.5),
        "conv_b": nrm(ks[16], (DEPTH, D_FF), 0.01),
        "w_down": nrm(ks[17], (DEPTH, D_FF, D_MODEL), D_FF ** -0.5),
        "norm_final_w": 1.0 + nrm(ks[18], (D_MODEL,), 0.01),
    }


def reference(x_prompt, x_sample, state_hgrn, state_ret, cache_conv, norm_mix_w, w_in, hg_lb_logits,
              hg_norm_w, ret_norm_w, ret_norm_b, w_branch, w_out, norm_ffn_w, w_up, conv_w, conv_b,
              w_down, norm_final_w):
    bp, tp, _ = x_prompt.shape
    bs, ts, _ = x_sample.shape
    pos_prompt = jnp.arange(tp, dtype=jnp.int32)
    pos_sample = PAST_LEN + jnp.arange(ts, dtype=jnp.int32)

    p_lb = jax.nn.softmax(hg_lb_logits.astype(jnp.float32), axis=0)
    lower_bounds = jnp.cumsum(p_lb, axis=0) - p_lb[0]

    xp, xs = x_prompt, x_sample
    hg_p, ret_p, conv_p, hg_s, ret_s, conv_s = [], [], [], [], [], []
    for l in range(DEPTH):
        weights = (lower_bounds[l], norm_mix_w[l], w_in[l], hg_norm_w[l], ret_norm_w[l], ret_norm_b[l],
                   w_branch[l], w_out[l], norm_ffn_w[l], w_up[l], conv_w[l], conv_b[l], w_down[l])
        zero_hg = jnp.zeros((bp, H_A, DH_A, DH_A), x_prompt.dtype)
        zero_ret = jnp.zeros((bp, H_B, DH_B, DH_B), x_prompt.dtype)
        zero_conv = jnp.zeros((bp, CONV_W - 1, D_FF), x_prompt.dtype)
        xp, s1, s2, s3 = trunk_layer(xp, pos_prompt, zero_hg, zero_ret, zero_conv, *weights)
        hg_p.append(s1); ret_p.append(s2); conv_p.append(s3)
        xs, s1, s2, s3 = trunk_layer(xs, pos_sample, state_hgrn[l], state_ret[l], cache_conv[l], *weights)
        hg_s.append(s1); ret_s.append(s2); conv_s.append(s3)

    y_prompt = rms_norm(xp, norm_final_w)
    y_sample = rms_norm(xs, norm_final_w)
    return (y_prompt, y_sample, jnp.stack(hg_p), jnp.stack(ret_p), jnp.stack(conv_p),
            jnp.stack(hg_s), jnp.stack(ret_s), jnp.stack(conv_s))
```

```python
import functools
import math

import numpy as np
import jax
import jax.numpy as jnp
from jax import lax
from jax.experimental import pallas as pl
from jax.experimental.pallas import tpu as pltpu

CHUNK = 64
SUB = 16
N_SUB = CHUNK // SUB
N_HEADS = 4
D_HEAD = 128
D_BRANCH = N_HEADS * D_HEAD
PAST_LEN = 2048
ROPE_BASE = 10000.0
NORM_EPS = 1e-6
GN_EPS = 1e-5
F_FLOOR = 1e-30
FFN_BLOCK = 256
VMEM_LIMIT_BYTES = 60 * 1024 * 1024

_F32 = jnp.float32
_BF16 = jnp.bfloat16


def _dot(a, b):
    return jnp.dot(a, b, preferred_element_type=_F32)


def _dot_nt(a, b):
    return lax.dot_general(a, b, (((1,), (1,)), ((), ())), preferred_element_type=_F32)


def _dot_tn(a, b):
    return lax.dot_general(a, b, (((0,), (0,)), ((), ())), preferred_element_type=_F32)


def _sigmoid(x):
    return jax.nn.sigmoid(x)


def _silu(x):
    return x * jax.nn.sigmoid(x)


def _rms_rows(x, w):
    ms = jnp.mean(x * x, axis=-1, keepdims=True)
    return x * lax.rsqrt(ms + NORM_EPS) * w


def _mixer_kernel(layer, nt, x_ref, shg_in_ref, sret_in_ref, nw_ref, win_ref, lbl_ref, hgw_ref,
                  rnw_ref, rnb_ref, wbr_ref, wout_ref, inv_ref, tril_ref, e4_ref,
                  y_ref, shg_ref, sret_ref,
                  h_s, q_s, k_s, bc_s, a_s, b_s, c_s, d_s, cat1_s, cat2_s, p_s, ad_s, bl_s, o_s, m_s):
    t = pl.program_id(0)
    nb = x_ref.shape[0]
    rows = nb * CHUNK
    dm = x_ref.shape[2]

    @pl.when(t == 0)
    def _():
        shg_ref[...] = jnp.zeros(shg_ref.shape, _F32)
        sret_ref[...] = jnp.zeros(sret_ref.shape, _F32)

    @pl.when(t == nt)
    def _():
        shg_ref[0] = shg_in_ref[...]
        sret_ref[0] = sret_in_ref[...]

    x = x_ref[...].reshape(rows, dm)
    h_s[...] = _rms_rows(x, nw_ref[...]).astype(_BF16)

    def proj(c0, width):
        return _dot(h_s[...], win_ref[:, c0:c0 + width])

    logits = lbl_ref[...]
    ex = jnp.exp(logits - jnp.max(logits, axis=0, keepdims=True))
    prob = ex / jnp.sum(ex, axis=0, keepdims=True)
    lrow = lax.broadcasted_iota(jnp.int32, prob.shape, 0)
    lb = jnp.sum(jnp.where((lrow >= 1) & (lrow <= layer), prob, 0.0), axis=0, keepdims=True)

    hq = proj(0, D_BRANCH)
    q_s[...] = _silu(hq)
    f_gate = lb + (1.0 - lb) * _sigmoid(proj(D_BRANCH, D_BRANCH))
    k_s[...] = 1.0 - f_gate
    log_f = jnp.log(jnp.maximum(f_gate, F_FLOOR))
    lf_hi = log_f.astype(_BF16)
    r1 = log_f - lf_hi.astype(_F32)
    lf_mid = r1.astype(_BF16)
    lf_lo = (r1 - lf_mid.astype(_F32)).astype(_BF16)
    tril = tril_ref[...]
    bc_s[...] = _dot(tril, lf_hi) + _dot(tril, lf_mid) + _dot(tril, lf_lo)
    d_s[...] = proj(2 * D_BRANCH, D_BRANCH).astype(_BF16)

    row3 = lax.broadcasted_iota(jnp.int32, (nb, CHUNK, D_HEAD), 1)
    ngrp = rows // SUB
    rowg = lax.broadcasted_iota(jnp.int32, (ngrp, SUB, D_HEAD), 1)
    ad_row = lax.broadcasted_iota(jnp.int32, (rows, D_HEAD), 0)
    ad_col = lax.broadcasted_iota(jnp.int32, (rows, D_HEAD), 1)
    ad_mask = (ad_col >> 4) == ((ad_row >> 4) & (N_SUB - 1))

    for hd in range(N_HEADS):
        sl = slice(hd * D_HEAD, (hd + 1) * D_HEAD)
        qh = q_s[:, sl]
        kh = k_s[:, sl]
        bh = bc_s[:, sl]
        q3 = qh.reshape(nb, CHUNK, D_HEAD)
        k3 = kh.reshape(nb, CHUNK, D_HEAD)
        b3 = bh.reshape(nb, CHUNK, D_HEAD)
        blast = b3[:, CHUNK - 1:CHUNK, :]
        a_s[:, sl] = (qh * jnp.exp(bh)).astype(_BF16)
        b_s[:, sl] = (k3 * jnp.exp(blast - b3)).reshape(rows, D_HEAD).astype(_BF16)
        bl_s[:, :, sl] = jnp.broadcast_to(jnp.exp(blast), (nb, 8, D_HEAD))
        for j in range(N_SUB - 1):
            ej = b3[:, SUB * j + SUB - 1:SUB * j + SUB, :]
            qj = jnp.where(row3 >= SUB * (j + 1), q3 * jnp.exp(jnp.minimum(b3 - ej, 0.0)), 0.0)
            kj = jnp.where((row3 >> 4) == j, k3 * jnp.exp(jnp.minimum(ej - b3, 0.0)), 0.0)
            c0 = (hd * (N_SUB - 1) + j) * D_HEAD
            cat1_s[:, c0:c0 + D_HEAD] = qj.reshape(rows, D_HEAD).astype(_BF16)
            cat2_s[:, c0:c0 + D_HEAD] = kj.reshape(rows, D_HEAD).astype(_BF16)
        qg = qh.reshape(ngrp, SUB, D_HEAD)
        kg = kh.reshape(ngrp, SUB, D_HEAD)
        bg = bh.reshape(ngrp, SUB, D_HEAD)
        for s in range(SUB):
            msk = rowg >= s
            dec = jnp.exp(jnp.where(msk, bg - bg[:, s:s + 1, :], 0.0))
            ps = jnp.where(msk, dec * qg * kg[:, s:s + 1, :], 0.0)
            p_s[:, s * D_HEAD:(s + 1) * D_HEAD] = ps.reshape(rows, D_HEAD).astype(_BF16)
        ad = _dot(p_s[...], e4_ref[...])
        ad_s[:, sl] = jnp.where(ad_mask, ad, 0.0)

    hgw = hgw_ref[...]

    def hg_body(b, carry):
        rs = pl.ds(pl.multiple_of(b * CHUNK, CHUNK), CHUNK)
        for hd in range(N_HEADS):
            sl = slice(hd * D_HEAD, (hd + 1) * D_HEAD)
            cs = slice(hd * (N_SUB - 1) * D_HEAD, (hd + 1) * (N_SUB - 1) * D_HEAD)
            st = shg_ref[0, b, hd]
            amat = _dot_nt(cat1_s[rs, cs], cat2_s[rs, cs]) + ad_s[rs, hd * D_HEAD:hd * D_HEAD + CHUNK]
            vv = d_s[rs, sl]
            o = _dot(amat.astype(_BF16), vv) + _dot_nt(a_s[rs, sl], st.astype(_BF16))
            shg_ref[0, b, hd] = st * bl_s[b, 0:1, sl] + _dot_tn(vv, b_s[rs, sl])
            o_s[rs, sl] = _rms_rows(o, hgw)
        return carry

    lax.fori_loop(0, nb, hg_body, 0)

    o_a = (o_s[...] * _silu(proj(3 * D_BRANCH, D_BRANCH))).astype(_BF16)
    z_a = _dot(o_a, wbr_ref[0])
    m_s[...] = _sigmoid(proj(8 * D_BRANCH, dm)) * z_a

    pos0 = jnp.where(t == nt, PAST_LEN, t * CHUNK)
    posf = (lax.broadcasted_iota(jnp.int32, (CHUNK, D_HEAD), 0) + pos0).astype(_F32)
    ang = posf * inv_ref[...]
    lane = lax.broadcasted_iota(jnp.int32, (CHUNK, D_HEAD), 1)
    cosf = jnp.cos(ang)
    sinf = jnp.where(lane < D_HEAD // 2, -1.0, 1.0) * jnp.sin(ang)
    cos3 = jnp.broadcast_to(cosf[None], (nb, CHUNK, D_HEAD)).reshape(rows, D_HEAD)
    sin3 = jnp.broadcast_to(sinf[None], (nb, CHUNK, D_HEAD)).reshape(rows, D_HEAD)

    def rope(v):
        return v * cos3 + pltpu.roll(v, D_HEAD // 2, 1) * sin3

    q_s[...] = proj(4 * D_BRANCH, D_BRANCH)
    k_s[...] = proj(5 * D_BRANCH, D_BRANCH)
    d_s[...] = proj(6 * D_BRANCH, D_BRANCH).astype(_BF16)

    idx_r = (lax.broadcasted_iota(jnp.int32, (rows, D_HEAD), 0) & (CHUNK - 1)).astype(_F32)
    idx_c = lax.broadcasted_iota(jnp.int32, (CHUNK, D_HEAD), 0).astype(_F32)
    rel = (lax.broadcasted_iota(jnp.int32, (CHUNK, CHUNK), 0)
           - lax.broadcasted_iota(jnp.int32, (CHUNK, CHUNK), 1)).astype(_F32)
    log_gammas = [math.log(1.0 - 2.0 ** (-5.0 - hd)) for hd in range(N_HEADS)]
    dmats, qdecs = [], []
    for hd in range(N_HEADS):
        sl = slice(hd * D_HEAD, (hd + 1) * D_HEAD)
        lg = log_gammas[hd]
        qh = rope(q_s[:, sl])
        kh = rope(k_s[:, sl]) * (D_HEAD ** -0.5)
        a_s[:, sl] = qh.astype(_BF16)
        b_s[:, sl] = kh.astype(_BF16)
        c_s[:, sl] = (kh * jnp.exp(lg * (CHUNK - 1.0 - idx_r))).astype(_BF16)
        dmats.append(jnp.where(rel >= 0, jnp.exp(lg * jnp.maximum(rel, 0.0)), 0.0))
        qdecs.append(jnp.exp(lg * (idx_c + 1.0)))

    rnw = rnw_ref[...]
    rnb = rnb_ref[...]

    def ret_body(b, carry):
        rs = pl.ds(pl.multiple_of(b * CHUNK, CHUNK), CHUNK)
        for hd in range(N_HEADS):
            sl = slice(hd * D_HEAD, (hd + 1) * D_HEAD)
            st = sret_ref[0, b, hd]
            qq = a_s[rs, sl]
            vv = d_s[rs, sl]
            amat = _dot_nt(qq, b_s[rs, sl]) * dmats[hd]
            o = _dot(amat.astype(_BF16), vv) + _dot(qq, st.astype(_BF16)) * qdecs[hd]
            sret_ref[0, b, hd] = math.exp(log_gammas[hd] * CHUNK) * st + _dot_tn(c_s[rs, sl], vv)
            mu = jnp.mean(o, axis=-1, keepdims=True)
            oc = o - mu
            var = jnp.mean(oc * oc, axis=-1, keepdims=True)
            o_s[rs, sl] = oc * lax.rsqrt(var + GN_EPS) * rnw[:, sl] + rnb[:, sl]
        return carry

    lax.fori_loop(0, nb, ret_body, 0)

    o_b = (o_s[...] * _silu(proj(7 * D_BRANCH, D_BRANCH))).astype(_BF16)
    z_b = _dot(o_b, wbr_ref[1])
    merged = m_s[...] + _sigmoid(proj(8 * D_BRANCH + dm, dm)) * z_b
    y = x_ref[...].reshape(rows, dm) + _dot(merged.astype(_BF16), wout_ref[...])
    y_ref[...] = y.reshape(y_ref.shape)


def _const_spec(block, index):
    return pl.BlockSpec(block, lambda t: index, pipeline_mode=pl.Buffered(1))


def _mixer_call(layer, nt, x, shg_in, sret_in, norm_w, w_in, lb_logits, hg_norm_w, ret_norm_w,
                ret_norm_b, w_branch, w_out, inv2, tril, e4):
    nb, tt, dm = x.shape
    depth = w_in.shape[0]
    rows = nb * CHUNK
    d_in = w_in.shape[2]
    st_block = (None, nb, N_HEADS, D_HEAD, D_HEAD)
    in_specs = [
        pl.BlockSpec((nb, CHUNK, dm), lambda t: (0, t, 0)),
        _const_spec(st_block, (layer, 0, 0, 0, 0)),
        _const_spec(st_block, (layer, 0, 0, 0, 0)),
        _const_spec((None, 1, dm), (layer, 0, 0)),
        _const_spec((None, dm, d_in), (layer, 0, 0)),
        _const_spec((depth, D_BRANCH), (0, 0)),
        _const_spec((None, 1, D_HEAD), (layer, 0, 0)),
        _const_spec((None, 1, D_BRANCH), (layer, 0, 0)),
        _const_spec((None, 1, D_BRANCH), (layer, 0, 0)),
        _const_spec((None, 2, D_BRANCH, dm), (layer, 0, 0, 0)),
        _const_spec((None, dm, dm), (layer, 0, 0)),
        _const_spec((1, D_HEAD), (0, 0)),
        _const_spec((rows, rows), (0, 0)),
        _const_spec((SUB * D_HEAD, D_HEAD), (0, 0)),
    ]
    st_out = jax.ShapeDtypeStruct((2, nb, N_HEADS, D_HEAD, D_HEAD), _F32)
    st_out_spec = pl.BlockSpec((1, nb, N_HEADS, D_HEAD, D_HEAD), lambda t: (t // nt, 0, 0, 0, 0))
    out_specs = [pl.BlockSpec((nb, CHUNK, dm), lambda t: (0, t, 0)), st_out_spec, st_out_spec]
    ncat = N_HEADS * (N_SUB - 1) * D_HEAD
    scratch = [
        pltpu.VMEM((rows, dm), _BF16),
        pltpu.VMEM((rows, D_BRANCH), _F32),
        pltpu.VMEM((rows, D_BRANCH), _F32),
        pltpu.VMEM((rows, D_BRANCH), _F32),
        pltpu.VMEM((rows, D_BRANCH), _BF16),
        pltpu.VMEM((rows, D_BRANCH), _BF16),
        pltpu.VMEM((rows, D_BRANCH), _BF16),
        pltpu.VMEM((rows, D_BRANCH), _BF16),
        pltpu.VMEM((rows, ncat), _BF16),
        pltpu.VMEM((rows, ncat), _BF16),
        pltpu.VMEM((rows, SUB * D_HEAD), _BF16),
        pltpu.VMEM((rows, D_BRANCH), _F32),
        pltpu.VMEM((nb, 8, D_BRANCH), _F32),
        pltpu.VMEM((rows, D_BRANCH), _F32),
        pltpu.VMEM((rows, dm), _F32),
    ]
    return pl.pallas_call(
        functools.partial(_mixer_kernel, layer, nt),
        grid=(nt + 1,),
        in_specs=in_specs,
        out_specs=out_specs,
        out_shape=[jax.ShapeDtypeStruct(x.shape, _F32), st_out, st_out],
        scratch_shapes=scratch,
        compiler_params=pltpu.CompilerParams(dimension_semantics=("arbitrary",),
                                             vmem_limit_bytes=VMEM_LIMIT_BYTES),
        name=f"mixer_l{layer}",
    )(x, shg_in, sret_in, norm_w, w_in, lb_logits, hg_norm_w, ret_norm_w, ret_norm_b, w_branch,
      w_out, inv2, tril, e4)


def _ffn_kernel(nt, final, x_ref, cc_in_ref, nw_ref, wa_ref, wg_ref, cw_ref, cb_ref, wd_ref, nfw_ref,
                y_ref, cc_ref, h_s, acc_s, ext_s):
    t = pl.program_id(0)
    nb = x_ref.shape[0]
    rows = nb * CHUNK
    dm = x_ref.shape[2]
    nblk = wa_ref.shape[0]
    fb = wa_ref.shape[2]

    @pl.when(t == 0)
    def _():
        cc_ref[...] = jnp.zeros(cc_ref.shape, _F32)

    @pl.when(t == nt)
    def _():
        cc_ref[0] = cc_in_ref[...]

    x = x_ref[...].reshape(rows, dm)
    h_s[...] = _rms_rows(x, nw_ref[...]).astype(_BF16)
    acc_s[...] = jnp.zeros(acc_s.shape, _F32)

    def body(i, carry):
        h = h_s[...]
        a3 = _dot(h, wa_ref[i]).reshape(nb, CHUNK, fb)
        g = _dot(h, wg_ref[i])
        ext_s[:, 8:8 + CHUNK, :] = a3
        ext_s[:, 6:8, :] = cc_ref[0, i]
        cw = cw_ref[i]
        conv = (cb_ref[i] + ext_s[:, 6:6 + CHUNK, :] * cw[0:1, :] + ext_s[:, 7:7 + CHUNK, :] * cw[1:2, :]
                + a3 * cw[2:3, :])
        cc_ref[0, i] = a3[:, CHUNK - 2:CHUNK, :]
        u = (_silu(conv).reshape(rows, fb) * g).astype(_BF16)
        acc_s[...] += _dot(u, wd_ref[i])
        return carry

    lax.fori_loop(0, nblk, body, 0)

    y = x_ref[...].reshape(rows, dm) + acc_s[...]
    if final:
        y = _rms_rows(y, nfw_ref[...])
    y_ref[...] = y.reshape(y_ref.shape)


def _ffn_call(layer, nt, final, x, cc_in, norm_w, wa, wg, cw, cb, wd, nfw):
    nb, tt, dm = x.shape
    rows = nb * CHUNK
    nblk, fb = wa.shape[1], wa.shape[3]
    in_specs = [
        pl.BlockSpec((nb, CHUNK, dm), lambda t: (0, t, 0)),
        _const_spec((None, nblk, nb, 2, fb), (layer, 0, 0, 0, 0)),
        _const_spec((None, 1, dm), (layer, 0, 0)),
        _const_spec((None, nblk, dm, fb), (layer, 0, 0, 0)),
        _const_spec((None, nblk, dm, fb), (layer, 0, 0, 0)),
        _const_spec((None, nblk, 3, fb), (layer, 0, 0, 0)),
        _const_spec((None, nblk, 1, fb), (layer, 0, 0, 0)),
        _const_spec((None, nblk, fb, dm), (layer, 0, 0, 0)),
        _const_spec((1, dm), (0, 0)),
    ]
    cc_out = jax.ShapeDtypeStruct((2, nblk, nb, 2, fb), _F32)
    out_specs = [pl.BlockSpec((nb, CHUNK, dm), lambda t: (0, t, 0)),
                 pl.BlockSpec((1, nblk, nb, 2, fb), lambda t: (t // nt, 0, 0, 0, 0))]
    scratch = [
        pltpu.VMEM((rows, dm), _BF16),
        pltpu.VMEM((rows, dm), _F32),
        pltpu.VMEM((nb, CHUNK + 8, fb), _F32),
    ]
    return pl.pallas_call(
        functools.partial(_ffn_kernel, nt, final),
        grid=(nt + 1,),
        in_specs=in_specs,
        out_specs=out_specs,
        out_shape=[jax.ShapeDtypeStruct(x.shape, _F32), cc_out],
        scratch_shapes=scratch,
        compiler_params=pltpu.CompilerParams(dimension_semantics=("arbitrary",),
                                             vmem_limit_bytes=VMEM_LIMIT_BYTES),
        name=f"ffn_l{layer}",
    )(x, cc_in, norm_w, wa, wg, cw, cb, wd, nfw)


def _constants(rows):
    half = D_HEAD // 2
    inv = ROPE_BASE ** (-jnp.arange(half, dtype=_F32) / half)
    inv2 = jnp.concatenate([inv, inv])[None, :]
    r = np.arange(rows)
    tril = ((r[:, None] // CHUNK == r[None, :] // CHUNK) & (r[:, None] >= r[None, :]))
    tril = jnp.asarray(tril, _BF16)
    e4 = np.zeros((SUB * D_HEAD, D_HEAD), np.float32)
    cols = np.arange(CHUNK)
    for s in range(SUB):
        e4[s * D_HEAD:(s + 1) * D_HEAD, cols[cols % SUB == s]] = 1.0
    return inv2, tril, jnp.asarray(e4, _BF16)


def kernel(x_prompt, x_sample, state_hgrn, state_ret, cache_conv, norm_mix_w, w_in, hg_lb_logits,
           hg_norm_w, ret_norm_w, ret_norm_b, w_branch, w_out, norm_ffn_w, w_up, conv_w, conv_b,
           w_down, norm_final_w):
    nb, tp, dm = x_prompt.shape
    depth = w_in.shape[0]
    d_ff = w_down.shape[1]
    assert x_sample.shape == (nb, CHUNK, dm) and tp % CHUNK == 0 and d_ff % FFN_BLOCK == 0
    nt = tp // CHUNK
    nblk = d_ff // FFN_BLOCK
    rows = nb * CHUNK
    inv2, tril, e4 = _constants(rows)

    w_in_b = w_in.astype(_BF16)
    w_branch_b = w_branch.astype(_BF16)
    w_out_b = w_out.astype(_BF16)
    w_up_b = w_up.astype(_BF16).reshape(depth, dm, 2, nblk, FFN_BLOCK)
    wa = jnp.transpose(w_up_b[:, :, 0], (0, 2, 1, 3))
    wg = jnp.transpose(w_up_b[:, :, 1], (0, 2, 1, 3))
    wd = w_down.astype(_BF16).reshape(depth, nblk, FFN_BLOCK, dm)
    cw = jnp.transpose(conv_w.reshape(depth, 3, nblk, FFN_BLOCK), (0, 2, 1, 3))
    cb = conv_b.reshape(depth, nblk, 1, FFN_BLOCK)
    cc_in = jnp.transpose(cache_conv.reshape(depth, nb, 2, nblk, FFN_BLOCK), (0, 3, 1, 2, 4))
    shg_in = jnp.swapaxes(state_hgrn, -1, -2)
    nmw = norm_mix_w[:, None, :]
    nfw_l = norm_ffn_w[:, None, :]
    hgw = hg_norm_w[:, None, :]
    rnw = ret_norm_w[:, None, :]
    rnb = ret_norm_b[:, None, :]
    nfin = norm_final_w[None, :]

    x = jnp.concatenate([x_prompt, x_sample], axis=1)
    hg_out, ret_out, cc_out = [], [], []
    for l in range(depth):
        x, s_hg, s_ret = _mixer_call(l, nt, x, shg_in, state_ret, nmw, w_in_b, hg_lb_logits, hgw, rnw, rnb,
                                     w_branch_b, w_out_b, inv2, tril, e4)
        x, s_cc = _ffn_call(l, nt, l == depth - 1, x, cc_in, nfw_l, wa, wg, cw, cb, wd, nfin)
        hg_out.append(s_hg)
        ret_out.append(s_ret)
        cc_out.append(s_cc)
    hg_all = jnp.swapaxes(jnp.stack(hg_out), -1, -2)
    ret_all = jnp.stack(ret_out)
    cc_all = jnp.transpose(jnp.stack(cc_out), (0, 1, 3, 4, 2, 5)).reshape(depth, 2, nb, 2, d_ff)
    return (x[:, :tp], x[:, tp:], hg_all[:, 0], ret_all[:, 0], cc_all[:, 0],
            hg_all[:, 1], ret_all[:, 1], cc_all[:, 1])
```

```python
import functools
import math

import numpy as np
import jax
import jax.numpy as jnp
from jax import lax
from jax.experimental import pallas as pl
from jax.experimental.pallas import tpu as pltpu

CHUNK = 64
SUB = 8
LEVELS = (64, 32, 16)
N_SLOT = sum(CHUNK // m for m in LEVELS)
N_HEADS = 4
D_HEAD = 128
D_BRANCH = N_HEADS * D_HEAD
PAST_LEN = 2048
ROPE_BASE = 10000.0
NORM_EPS = 1e-6
GN_EPS = 1e-5
F_FLOOR = 1e-30
FFN_BLOCK = 256
VMEM_LIMIT_BYTES = 60 * 1024 * 1024

_F32 = jnp.float32
_BF16 = jnp.bfloat16


def _dot(a, b):
    return jnp.dot(a, b, preferred_element_type=_F32)


def _dot_nt(a, b):
    return lax.dot_general(a, b, (((1,), (1,)), ((), ())), preferred_element_type=_F32)


def _dot_tn(a, b):
    return lax.dot_general(a, b, (((0,), (0,)), ((), ())), preferred_element_type=_F32)


def _sigmoid(x):
    return jax.nn.sigmoid(x)


def _silu(x):
    return x * jax.nn.sigmoid(x)


def _rms_rows(x, w):
    ms = jnp.mean(x * x, axis=-1, keepdims=True)
    return x * lax.rsqrt(ms + NORM_EPS) * w


def _mixer_kernel(nt, l_ref, x_ref, shg_in_ref, sret_in_ref, nw_ref, win_ref, lbl_ref, hgw_ref,
                  rnw_ref, rnb_ref, wbr_ref, wout_ref, inv_ref, tril_ref, e8_ref,
                  y_ref, shg_ref, sret_ref,
                  h_s, q_s, k_s, bc_s, v_s, rq_s, rk_s, rv_s, sga_s, sgr_s, oa_s, ob_s, za_s, m_s, gb_s,
                  cat1_s, cat2_s, p_s):
    t = pl.program_id(0)
    layer = l_ref[0]
    nb = x_ref.shape[0]
    rows = nb * CHUNK
    dm = x_ref.shape[2]
    seq_rows = [slice(b * CHUNK, (b + 1) * CHUNK) for b in range(nb)]

    @pl.when(t == 0)
    def _():
        shg_ref[...] = jnp.zeros(shg_ref.shape, _F32)
        sret_ref[...] = jnp.zeros(sret_ref.shape, _F32)

    @pl.when(t == nt)
    def _():
        shg_ref[0] = shg_in_ref[...]
        sret_ref[0] = sret_in_ref[...]

    x = x_ref[...].reshape(rows, dm)
    h_s[...] = _rms_rows(x, nw_ref[...]).astype(_BF16)

    def proj(group, width=D_BRANCH, offset=0):
        c0 = group * D_BRANCH + offset
        return _dot(h_s[...], win_ref[:, c0:c0 + width])

    logits = lbl_ref[...]
    ex = jnp.exp(logits - jnp.max(logits, axis=0, keepdims=True))
    prob = ex / jnp.sum(ex, axis=0, keepdims=True)
    lrow = lax.broadcasted_iota(jnp.int32, prob.shape, 0)
    lb = jnp.sum(jnp.where((lrow >= 1) & (lrow <= layer), prob, 0.0), axis=0, keepdims=True)

    q_s[...] = _silu(proj(0))
    f_gate = lb + (1.0 - lb) * _sigmoid(proj(1))
    k_s[...] = 1.0 - f_gate
    log_f = jnp.log(jnp.maximum(f_gate, F_FLOOR))
    lf_hi = log_f.astype(_BF16)
    r1 = log_f - lf_hi.astype(_F32)
    lf_mid = r1.astype(_BF16)
    lf_lo = (r1 - lf_mid.astype(_F32)).astype(_BF16)
    tril3 = tril_ref[...]
    for rs in seq_rows:
        bc_s[rs, :] = _dot(tril3, jnp.concatenate([lf_hi[rs], lf_mid[rs], lf_lo[rs]], axis=0))
    v_s[...] = proj(2).astype(_BF16)

    jrow = lax.broadcasted_iota(jnp.int32, (rows, D_HEAD), 0) & (CHUNK - 1)
    rowg = lax.broadcasted_iota(jnp.int32, (rows // SUB, SUB, D_HEAD), 1)
    ad_col = lax.broadcasted_iota(jnp.int32, (rows, D_HEAD), 1)
    ad_mask = (ad_col >> 3) == (jrow >> 3)
    hgw = hgw_ref[...]

    def dense_rq():
        rq_s[...] = proj(4)

    def dense_rk():
        rk_s[...] = proj(5)

    def dense_rv():
        rv_s[...] = proj(6).astype(_BF16)

    def dense_hgate():
        sga_s[...] = _silu(proj(3))

    hg_dense = (dense_rq, dense_rk, dense_rv, dense_hgate)

    for hd in range(N_HEADS):
        hg_dense[hd]()
        sl = slice(hd * D_HEAD, (hd + 1) * D_HEAD)
        cat1 = cat1_s.at[hd % 2]
        cat2 = cat2_s.at[hd % 2]
        p_buf = p_s.at[hd % 2]
        qh = q_s[:, sl]
        kh = k_s[:, sl]
        bh = bc_s[:, sl]
        vh = v_s[:, sl]
        b3 = bh.reshape(nb, CHUNK, D_HEAD)
        blast = b3[:, CHUNK - 1:CHUNK, :]
        qe = (qh * jnp.exp(bh)).astype(_BF16)
        kd = (kh.reshape(b3.shape) * jnp.exp(blast - b3)).reshape(rows, D_HEAD).astype(_BF16)
        chunk_dec = jnp.exp(blast)
        slot = 0
        for m in LEVELS:
            half = m // 2
            bm = bh.reshape(rows // m, m, D_HEAD)
            delta = bm - bm[:, half - 1:half, :]
            is_q = lax.broadcasted_iota(jnp.int32, bm.shape, 1) >= half
            dec = jnp.exp(jnp.minimum(jnp.where(is_q, delta, -delta), 0.0))
            xm = (jnp.where(is_q, qh.reshape(bm.shape), kh.reshape(bm.shape)) * dec).reshape(rows, D_HEAD)
            hidx = jrow >> int(math.log2(half))
            for s in range(CHUNK // m):
                cs = slice(slot * D_HEAD, (slot + 1) * D_HEAD)
                cat1[:, cs] = jnp.where(hidx == 2 * s + 1, xm, 0.0).astype(_BF16)
                cat2[:, cs] = jnp.where(hidx == 2 * s, xm, 0.0).astype(_BF16)
                slot += 1
        qg = qh.reshape(rowg.shape)
        kg = kh.reshape(rowg.shape)
        bg = bh.reshape(rowg.shape)
        for s in range(SUB):
            msk = rowg >= s
            dec = jnp.exp(jnp.where(msk, bg - bg[:, s:s + 1, :], 0.0))
            ps = jnp.where(msk, dec * qg * kg[:, s:s + 1, :], 0.0)
            p_buf[:, s * D_HEAD:(s + 1) * D_HEAD] = ps.reshape(rows, D_HEAD).astype(_BF16)
        ad = jnp.where(ad_mask, _dot(p_buf[...], e8_ref[...]), 0.0)
        states = [shg_ref[0, b, hd] for b in range(nb)]
        amats = [_dot_nt(cat1[rs, :], cat2[rs, :]) + ad[rs, 0:CHUNK] for rs in seq_rows]
        outs = [_dot(amats[b].astype(_BF16), vh[rs]) + _dot_nt(qe[rs], states[b].astype(_BF16))
                for b, rs in enumerate(seq_rows)]
        for b, rs in enumerate(seq_rows):
            shg_ref[0, b, hd] = states[b] * chunk_dec[b] + _dot_tn(vh[rs], kd[rs])
        oa_s[:, sl] = _rms_rows(jnp.concatenate(outs, axis=0), hgw)

    pos0 = jnp.where(t == nt, PAST_LEN, t * CHUNK)
    posf = (lax.broadcasted_iota(jnp.int32, (CHUNK, D_HEAD), 0) + pos0).astype(_F32)
    ang = posf * inv_ref[...]
    lane = lax.broadcasted_iota(jnp.int32, (CHUNK, D_HEAD), 1)
    cosf = jnp.cos(ang)
    sinf = jnp.where(lane < D_HEAD // 2, -1.0, 1.0) * jnp.sin(ang)
    cos3 = jnp.broadcast_to(cosf[None], (nb, CHUNK, D_HEAD)).reshape(rows, D_HEAD)
    sin3 = jnp.broadcast_to(sinf[None], (nb, CHUNK, D_HEAD)).reshape(rows, D_HEAD)

    def rope(v):
        return v * cos3 + pltpu.roll(v, D_HEAD // 2, 1) * sin3

    def dense_rgate_za():
        sgr_s[...] = _silu(proj(7))
        za_s[...] = _dot((oa_s[...] * sga_s[...]).astype(_BF16), wbr_ref[0])

    def dense_gate_a():
        m_s[...] = _sigmoid(proj(8, dm)) * za_s[...]

    def dense_gate_b0():
        gb_s[:, 0:dm // 2] = _sigmoid(proj(8, dm // 2, dm))

    def dense_gate_b1():
        gb_s[:, dm // 2:dm] = _sigmoid(proj(8, dm // 2, dm + dm // 2))

    ret_dense = (dense_rgate_za, dense_gate_a, dense_gate_b0, dense_gate_b1)

    idx_r = jrow.astype(_F32)
    idx_c = lax.broadcasted_iota(jnp.int32, (CHUNK, D_HEAD), 0).astype(_F32)
    rel = (lax.broadcasted_iota(jnp.int32, (CHUNK, CHUNK), 0)
           - lax.broadcasted_iota(jnp.int32, (CHUNK, CHUNK), 1)).astype(_F32)
    rnw = rnw_ref[...]
    rnb = rnb_ref[...]
    for hd in range(N_HEADS):
        ret_dense[hd]()
        sl = slice(hd * D_HEAD, (hd + 1) * D_HEAD)
        lg = math.log(1.0 - 2.0 ** (-5.0 - hd))
        kf = rope(rk_s[:, sl]) * (D_HEAD ** -0.5)
        qb = rope(rq_s[:, sl]).astype(_BF16)
        kb = kf.astype(_BF16)
        kdec = (kf * jnp.exp(lg * (CHUNK - 1.0 - idx_r))).astype(_BF16)
        vh = rv_s[:, sl]
        dmat = jnp.where(rel >= 0, jnp.exp(lg * jnp.maximum(rel, 0.0)), 0.0)
        qdec = jnp.exp(lg * (idx_c + 1.0))
        states = [sret_ref[0, b, hd] for b in range(nb)]
        amats = [_dot_nt(qb[rs], kb[rs]) * dmat for rs in seq_rows]
        outs = [_dot(amats[b].astype(_BF16), vh[rs]) + _dot(qb[rs], states[b].astype(_BF16)) * qdec
                for b, rs in enumerate(seq_rows)]
        for b, rs in enumerate(seq_rows):
            sret_ref[0, b, hd] = math.exp(lg * CHUNK) * states[b] + _dot_tn(kdec[rs], vh[rs])
        o = jnp.concatenate(outs, axis=0)
        mu = jnp.mean(o, axis=-1, keepdims=True)
        oc = o - mu
        var = jnp.mean(oc * oc, axis=-1, keepdims=True)
        ob_s[:, sl] = oc * lax.rsqrt(var + GN_EPS) * rnw[:, sl] + rnb[:, sl]

    z_b = _dot((ob_s[...] * sgr_s[...]).astype(_BF16), wbr_ref[1])
    merged = m_s[...] + gb_s[...] * z_b
    y = x_ref[...].reshape(rows, dm) + _dot(merged.astype(_BF16), wout_ref[...])
    y_ref[...] = y.reshape(y_ref.shape)


def _layer_spec(block):
    zeros = (0,) * (len(block) - 1)
    return pl.BlockSpec(block, lambda t, l: (l[0],) + zeros, pipeline_mode=pl.Buffered(1))


def _const_spec(block):
    zeros = (0,) * len(block)
    return pl.BlockSpec(block, lambda t, l: zeros, pipeline_mode=pl.Buffered(1))


def _mixer_call(nt, lv, x, shg_in, sret_in, norm_w, w_in, lb_logits, hg_norm_w, ret_norm_w,
                ret_norm_b, w_branch, w_out, inv2, tril3, e8):
    nb, tt, dm = x.shape
    depth = w_in.shape[0]
    rows = nb * CHUNK
    d_in = w_in.shape[2]
    st_block = (None, nb, N_HEADS, D_HEAD, D_HEAD)
    x_spec = pl.BlockSpec((nb, CHUNK, dm), lambda t, l: (0, t, 0))
    in_specs = [
        x_spec,
        _layer_spec(st_block),
        _layer_spec(st_block),
        _layer_spec((None, 1, dm)),
        _layer_spec((None, dm, d_in)),
        _const_spec((depth, D_BRANCH)),
        _layer_spec((None, 1, D_HEAD)),
        _layer_spec((None, 1, D_BRANCH)),
        _layer_spec((None, 1, D_BRANCH)),
        _layer_spec((None, 2, D_BRANCH, dm)),
        _layer_spec((None, dm, dm)),
        _const_spec((1, D_HEAD)),
        _const_spec((CHUNK, 3 * CHUNK)),
        _const_spec((SUB * D_HEAD, D_HEAD)),
    ]
    st_out = jax.ShapeDtypeStruct((2, nb, N_HEADS, D_HEAD, D_HEAD), _F32)
    st_out_spec = pl.BlockSpec((1, nb, N_HEADS, D_HEAD, D_HEAD), lambda t, l: (t // nt, 0, 0, 0, 0))
    scratch = [
        pltpu.VMEM((rows, dm), _BF16),
        pltpu.VMEM((rows, D_BRANCH), _F32),
        pltpu.VMEM((rows, D_BRANCH), _F32),
        pltpu.VMEM((rows, D_BRANCH), _F32),
        pltpu.VMEM((rows, D_BRANCH), _BF16),
        pltpu.VMEM((rows, D_BRANCH), _F32),
        pltpu.VMEM((rows, D_BRANCH), _F32),
        pltpu.VMEM((rows, D_BRANCH), _BF16),
        pltpu.VMEM((rows, D_BRANCH), _F32),
        pltpu.VMEM((rows, D_BRANCH), _F32),
        pltpu.VMEM((rows, D_BRANCH), _F32),
        pltpu.VMEM((rows, D_BRANCH), _F32),
        pltpu.VMEM((rows, dm), _F32),
        pltpu.VMEM((rows, dm), _F32),
        pltpu.VMEM((rows, dm), _F32),
        pltpu.VMEM((2, rows, N_SLOT * D_HEAD), _BF16),
        pltpu.VMEM((2, rows, N_SLOT * D_HEAD), _BF16),
        pltpu.VMEM((2, rows, SUB * D_HEAD), _BF16),
    ]
    grid_spec = pltpu.PrefetchScalarGridSpec(
        num_scalar_prefetch=1, grid=(nt + 1,), in_specs=in_specs,
        out_specs=[x_spec, st_out_spec, st_out_spec], scratch_shapes=scratch)
    return pl.pallas_call(
        functools.partial(_mixer_kernel, nt),
        grid_spec=grid_spec,
        out_shape=[jax.ShapeDtypeStruct(x.shape, _F32), st_out, st_out],
        input_output_aliases={1: 0},
        compiler_params=pltpu.CompilerParams(dimension_semantics=("arbitrary",),
                                             vmem_limit_bytes=VMEM_LIMIT_BYTES),
        name="mixer",
    )(lv, x, shg_in, sret_in, norm_w, w_in, lb_logits, hg_norm_w, ret_norm_w, ret_norm_b, w_branch,
      w_out, inv2, tril3, e8)


def _ffn_kernel(nt, depth, l_ref, x_ref, cc_in_ref, nw_ref, wa_ref, wg_ref, cw_ref, cb_ref, wd_ref, nfw_ref,
                y_ref, cc_ref, h_s, acc_s, ext_s, g_s):
    t = pl.program_id(0)
    layer = l_ref[0]
    nb = x_ref.shape[0]
    rows = nb * CHUNK
    dm = x_ref.shape[2]
    nblk = wa_ref.shape[0]
    fb = wa_ref.shape[2]

    @pl.when(t == 0)
    def _():
        cc_ref[...] = jnp.zeros(cc_ref.shape, _F32)

    @pl.when(t == nt)
    def _():
        cc_ref[0] = cc_in_ref[...]

    x = x_ref[...].reshape(rows, dm)
    h_s[...] = _rms_rows(x, nw_ref[...]).astype(_BF16)

    def up(i):
        h = h_s[...]
        ext_s[i % 2, :, 8:8 + CHUNK, :] = _dot(h, wa_ref[i]).reshape(nb, CHUNK, fb)
        g_s[i % 2] = _dot(h, wg_ref[i])

    def down(i):
        ext = ext_s.at[i % 2]
        ext[:, 6:8, :] = cc_ref[0, i]
        a3 = ext[:, 8:8 + CHUNK, :]
        cw = cw_ref[i]
        conv = (cb_ref[i] + ext[:, 6:6 + CHUNK, :] * cw[0:1, :] + ext[:, 7:7 + CHUNK, :] * cw[1:2, :]
                + a3 * cw[2:3, :])
        cc_ref[0, i] = a3[:, CHUNK - 2:CHUNK, :]
        u = (_silu(conv).reshape(rows, fb) * g_s[i % 2]).astype(_BF16)
        part = _dot(u, wd_ref[i])
        if i == 0:
            acc_s[...] = part
        else:
            acc_s[...] += part

    up(0)
    for i in range(nblk):
        if i + 1 < nblk:
            up(i + 1)
        down(i)

    y = x_ref[...].reshape(rows, dm) + acc_s[...]

    @pl.when(layer == depth - 1)
    def _():
        y_ref[...] = _rms_rows(y, nfw_ref[...]).reshape(y_ref.shape)

    @pl.when(layer != depth - 1)
    def _():
        y_ref[...] = y.reshape(y_ref.shape)


def _ffn_call(nt, lv, x, cc_in, norm_w, wa, wg, cw, cb, wd, nfw):
    nb, tt, dm = x.shape
    rows = nb * CHUNK
    depth, nblk, fb = wa.shape[0], wa.shape[1], wa.shape[3]
    x_spec = pl.BlockSpec((nb, CHUNK, dm), lambda t, l: (0, t, 0))
    in_specs = [
        x_spec,
        _layer_spec((None, nblk, nb, 2, fb)),
        _layer_spec((None, 1, dm)),
        _layer_spec((None, nblk, dm, fb)),
        _layer_spec((None, nblk, dm, fb)),
        _layer_spec((None, nblk, 3, fb)),
        _layer_spec((None, nblk, 1, fb)),
        _layer_spec((None, nblk, fb, dm)),
        _const_spec((1, dm)),
    ]
    cc_out = jax.ShapeDtypeStruct((2, nblk, nb, 2, fb), _F32)
    cc_spec = pl.BlockSpec((1, nblk, nb, 2, fb), lambda t, l: (t // nt, 0, 0, 0, 0))
    scratch = [
        pltpu.VMEM((rows, dm), _BF16),
        pltpu.VMEM((rows, dm), _F32),
        pltpu.VMEM((2, nb, CHUNK + 8, fb), _F32),
        pltpu.VMEM((2, rows, fb), _F32),
    ]
    grid_spec = pltpu.PrefetchScalarGridSpec(
        num_scalar_prefetch=1, grid=(nt + 1,), in_specs=in_specs,
        out_specs=[x_spec, cc_spec], scratch_shapes=scratch)
    return pl.pallas_call(
        functools.partial(_ffn_kernel, nt, depth),
        grid_spec=grid_spec,
        out_shape=[jax.ShapeDtypeStruct(x.shape, _F32), cc_out],
        input_output_aliases={1: 0},
        compiler_params=pltpu.CompilerParams(dimension_semantics=("arbitrary",),
                                             vmem_limit_bytes=VMEM_LIMIT_BYTES),
        name="ffn",
    )(lv, x, cc_in, norm_w, wa, wg, cw, cb, wd, nfw)


def _constants():
    half = D_HEAD // 2
    inv = ROPE_BASE ** (-jnp.arange(half, dtype=_F32) / half)
    inv2 = jnp.concatenate([inv, inv])[None, :]
    r = np.arange(CHUNK)
    tril = (r[:, None] >= r[None, :]).astype(np.float32)
    tril3 = jnp.asarray(np.concatenate([tril, tril, tril], axis=1), _BF16)
    e8 = np.zeros((SUB * D_HEAD, D_HEAD), np.float32)
    cols = np.arange(CHUNK)
    for s in range(SUB):
        e8[s * D_HEAD:(s + 1) * D_HEAD, cols[cols % SUB == s]] = 1.0
    return inv2, tril3, jnp.asarray(e8, _BF16)


def kernel(x_prompt, x_sample, state_hgrn, state_ret, cache_conv, norm_mix_w, w_in, hg_lb_logits,
           hg_norm_w, ret_norm_w, ret_norm_b, w_branch, w_out, norm_ffn_w, w_up, conv_w, conv_b,
           w_down, norm_final_w):
    nb, tp, dm = x_prompt.shape
    depth = w_in.shape[0]
    d_ff = w_down.shape[1]
    assert x_sample.shape == (nb, CHUNK, dm) and tp % CHUNK == 0 and d_ff % FFN_BLOCK == 0
    nt = tp // CHUNK
    nblk = d_ff // FFN_BLOCK
    inv2, tril3, e8 = _constants()

    w_in_b = w_in.astype(_BF16)
    w_branch_b = w_branch.astype(_BF16)
    w_out_b = w_out.astype(_BF16)
    w_up_b = w_up.astype(_BF16).reshape(depth, dm, 2, nblk, FFN_BLOCK)
    wa = jnp.transpose(w_up_b[:, :, 0], (0, 2, 1, 3))
    wg = jnp.transpose(w_up_b[:, :, 1], (0, 2, 1, 3))
    wd = w_down.astype(_BF16).reshape(depth, nblk, FFN_BLOCK, dm)
    cw = jnp.transpose(conv_w.reshape(depth, 3, nblk, FFN_BLOCK), (0, 2, 1, 3))
    cb = conv_b.reshape(depth, nblk, 1, FFN_BLOCK)
    cc_in = jnp.transpose(cache_conv.reshape(depth, nb, 2, nblk, FFN_BLOCK), (0, 3, 1, 2, 4))
    shg_in = jnp.swapaxes(state_hgrn, -1, -2)
    nmw = norm_mix_w[:, None, :]
    nfw_l = norm_ffn_w[:, None, :]
    hgw = hg_norm_w[:, None, :]
    rnw = ret_norm_w[:, None, :]
    rnb = ret_norm_b[:, None, :]
    nfin = norm_final_w[None, :]

    def layer_fn(x, l):
        lv = jnp.reshape(l, (1,))
        x, s_hg, s_ret = _mixer_call(nt, lv, x, shg_in, state_ret, nmw, w_in_b, hg_lb_logits, hgw, rnw, rnb,
                                     w_branch_b, w_out_b, inv2, tril3, e8)
        x, s_cc = _ffn_call(nt, lv, x, cc_in, nfw_l, wa, wg, cw, cb, wd, nfin)
        return x, (s_hg, s_ret, s_cc)

    x = jnp.concatenate([x_prompt, x_sample], axis=1)
    x, (hg_all, ret_all, cc_all) = lax.scan(layer_fn, x, jnp.arange(depth, dtype=jnp.int32))
    hg_all = jnp.swapaxes(hg_all, -1, -2)
    cc_all = jnp.transpose(cc_all, (0, 1, 3, 4, 2, 5)).reshape(depth, 2, nb, 2, d_ff)
    return (x[:, :tp], x[:, tp:], hg_all[:, 0], ret_all[:, 0], cc_all[:, 0],
            hg_all[:, 1], ret_all[:, 1], cc_all[:, 1])
```

```python
import functools
import math

import numpy as np
import jax
import jax.numpy as jnp
from jax import lax
from jax.experimental import pallas as pl
from jax.experimental.pallas import tpu as pltpu

CHUNK = 64
SUB = 8
LEVELS = (64, 32, 16)
N_SLOT = sum(CHUNK // m for m in LEVELS)
N_HEADS = 4
D_HEAD = 128
D_BRANCH = N_HEADS * D_HEAD
PAST_LEN = 2048
ROPE_BASE = 10000.0
NORM_EPS = 1e-6
GN_EPS = 1e-5
F_FLOOR = 1e-30
FFN_BLOCK = 256
VMEM_LIMIT_BYTES = 60 * 1024 * 1024

_F32 = jnp.float32
_BF16 = jnp.bfloat16


def _dot(a, b):
    return jnp.dot(a, b, preferred_element_type=_F32)


def _dot_nt(a, b):
    return lax.dot_general(a, b, (((1,), (1,)), ((), ())), preferred_element_type=_F32)


def _dot_tn(a, b):
    return lax.dot_general(a, b, (((0,), (0,)), ((), ())), preferred_element_type=_F32)


def _sigmoid(x):
    return jax.nn.sigmoid(x)


def _silu(x):
    return x * jax.nn.sigmoid(x)


def _rms_rows(x, w):
    ms = jnp.mean(x * x, axis=-1, keepdims=True)
    return x * lax.rsqrt(ms + NORM_EPS) * w


def _mixer_kernel(nt, l_ref, x_ref, shg_in_ref, sret_in_ref, nw_ref, win_ref, lbl_ref, hgw_ref,
                  rnw_ref, rnb_ref, wbr_ref, wout_ref, inv_ref, tril_ref, e8_ref,
                  y_ref, shg_ref, sret_ref,
                  h_s, q_s, k_s, bc_s, v_s, rq_s, rk_s, rv_s, sga_s, sgr_s, oa_s, ob_s, za_s, m_s, gb_s,
                  cat1_s, cat2_s, p_s):
    t = pl.program_id(0)
    layer = l_ref[0]
    nb = x_ref.shape[0]
    rows = nb * CHUNK
    dm = x_ref.shape[2]
    seq_rows = [slice(b * CHUNK, (b + 1) * CHUNK) for b in range(nb)]

    @pl.when(t == 0)
    def _():
        shg_ref[...] = jnp.zeros(shg_ref.shape, _F32)
        sret_ref[...] = jnp.zeros(sret_ref.shape, _F32)

    @pl.when(t == nt)
    def _():
        shg_ref[0] = shg_in_ref[...]
        sret_ref[0] = sret_in_ref[...]

    x = x_ref[...].reshape(rows, dm)
    h_s[...] = _rms_rows(x, nw_ref[...]).astype(_BF16)

    def proj(group, width=D_BRANCH, offset=0):
        c0 = group * D_BRANCH + offset
        return _dot(h_s[...], win_ref[:, c0:c0 + width])

    logits = lbl_ref[...]
    ex = jnp.exp(logits - jnp.max(logits, axis=0, keepdims=True))
    prob = ex / jnp.sum(ex, axis=0, keepdims=True)
    lrow = lax.broadcasted_iota(jnp.int32, prob.shape, 0)
    lb = jnp.sum(jnp.where((lrow >= 1) & (lrow <= layer), prob, 0.0), axis=0, keepdims=True)

    q_s[...] = _silu(proj(0))
    f_gate = lb + (1.0 - lb) * _sigmoid(proj(1))
    k_s[...] = 1.0 - f_gate
    log_f = jnp.log(jnp.maximum(f_gate, F_FLOOR))
    lf_hi = log_f.astype(_BF16)
    r1 = log_f - lf_hi.astype(_F32)
    lf_mid = r1.astype(_BF16)
    lf_lo = (r1 - lf_mid.astype(_F32)).astype(_BF16)
    tril3 = tril_ref[...]
    for rs in seq_rows:
        bc_s[rs, :] = _dot(tril3, jnp.concatenate([lf_hi[rs], lf_mid[rs], lf_lo[rs]], axis=0))
    v_s[...] = proj(2).astype(_BF16)

    jrow = lax.broadcasted_iota(jnp.int32, (rows, D_HEAD), 0) & (CHUNK - 1)
    rowg = lax.broadcasted_iota(jnp.int32, (rows // SUB, SUB, D_HEAD), 1)
    ad_col = lax.broadcasted_iota(jnp.int32, (rows, D_HEAD), 1)
    ad_mask = (ad_col >> 3) == (jrow >> 3)
    hgw = hgw_ref[...]

    def dense_rq():
        rq_s[...] = proj(4)

    def dense_rk():
        rk_s[...] = proj(5)

    def dense_rv():
        rv_s[...] = proj(6).astype(_BF16)

    def dense_hgate():
        sga_s[...] = _silu(proj(3))

    hg_dense = (dense_rq, dense_rk, dense_rv, dense_hgate)

    for hd in range(N_HEADS):
        hg_dense[hd]()
        sl = slice(hd * D_HEAD, (hd + 1) * D_HEAD)
        cat1 = cat1_s.at[hd]
        cat2 = cat2_s.at[hd]
        p_buf = p_s.at[hd % 2]
        qh = q_s[:, sl]
        kh = k_s[:, sl]
        bh = bc_s[:, sl]
        vh = v_s[:, sl]
        b3 = bh.reshape(nb, CHUNK, D_HEAD)
        blast = b3[:, CHUNK - 1:CHUNK, :]
        qe = (qh * jnp.exp(bh)).astype(_BF16)
        kd = (kh.reshape(b3.shape) * jnp.exp(blast - b3)).reshape(rows, D_HEAD).astype(_BF16)
        chunk_dec = jnp.exp(blast)
        slot = 0
        for m in LEVELS:
            half = m // 2
            bm = bh.reshape(rows // m, m, D_HEAD)
            delta = bm - bm[:, half - 1:half, :]
            is_q = lax.broadcasted_iota(jnp.int32, bm.shape, 1) >= half
            dec = jnp.exp(jnp.minimum(jnp.where(is_q, delta, -delta), 0.0))
            xm = (jnp.where(is_q, qh.reshape(bm.shape), kh.reshape(bm.shape)) * dec).reshape(rows, D_HEAD)
            hidx = jrow >> int(math.log2(half))
            for s in range(CHUNK // m):
                cs = slice(slot * D_HEAD, (slot + 1) * D_HEAD)
                cat1[:, cs] = jnp.where(hidx == 2 * s + 1, xm, 0.0).astype(_BF16)
                cat2[:, cs] = jnp.where(hidx == 2 * s, xm, 0.0).astype(_BF16)
                slot += 1
        qg = qh.reshape(rowg.shape)
        kg = kh.reshape(rowg.shape)
        bg = bh.reshape(rowg.shape)
        for s in range(SUB):
            msk = rowg >= s
            dec = jnp.exp(jnp.where(msk, bg - bg[:, s:s + 1, :], 0.0))
            ps = jnp.where(msk, dec * qg * kg[:, s:s + 1, :], 0.0)
            p_buf[:, s * D_HEAD:(s + 1) * D_HEAD] = ps.reshape(rows, D_HEAD).astype(_BF16)
        ad = jnp.where(ad_mask, _dot(p_buf[...], e8_ref[...]), 0.0)
        states = [shg_ref[0, b, hd] for b in range(nb)]
        amats = [_dot_nt(cat1[rs, :], cat2[rs, :]) + ad[rs, 0:CHUNK] for rs in seq_rows]
        outs = [_dot(amats[b].astype(_BF16), vh[rs]) + _dot_nt(qe[rs], states[b].astype(_BF16))
                for b, rs in enumerate(seq_rows)]
        for b, rs in enumerate(seq_rows):
            shg_ref[0, b, hd] = states[b] * chunk_dec[b] + _dot_tn(vh[rs], kd[rs])
        oa_s[:, sl] = _rms_rows(jnp.concatenate(outs, axis=0), hgw)

    pos0 = jnp.where(t == nt, PAST_LEN, t * CHUNK)
    posf = (lax.broadcasted_iota(jnp.int32, (CHUNK, D_HEAD), 0) + pos0).astype(_F32)
    ang = posf * inv_ref[...]
    lane = lax.broadcasted_iota(jnp.int32, (CHUNK, D_HEAD), 1)
    cosf = jnp.cos(ang)
    sinf = jnp.where(lane < D_HEAD // 2, -1.0, 1.0) * jnp.sin(ang)
    cos3 = jnp.broadcast_to(cosf[None], (nb, CHUNK, D_HEAD)).reshape(rows, D_HEAD)
    sin3 = jnp.broadcast_to(sinf[None], (nb, CHUNK, D_HEAD)).reshape(rows, D_HEAD)

    def rope(v):
        return v * cos3 + pltpu.roll(v, D_HEAD // 2, 1) * sin3

    def dense_rgate_za():
        sgr_s[...] = _silu(proj(7))
        za_s[...] = _dot((oa_s[...] * sga_s[...]).astype(_BF16), wbr_ref[0])

    def dense_gate_a():
        m_s[...] = _sigmoid(proj(8, dm)) * za_s[...]

    def dense_gate_b0():
        gb_s[:, 0:dm // 2] = _sigmoid(proj(8, dm // 2, dm))

    def dense_gate_b1():
        gb_s[:, dm // 2:dm] = _sigmoid(proj(8, dm // 2, dm + dm // 2))

    ret_dense = (dense_rgate_za, dense_gate_a, dense_gate_b0, dense_gate_b1)

    idx_r = jrow.astype(_F32)
    idx_c = lax.broadcasted_iota(jnp.int32, (CHUNK, D_HEAD), 0).astype(_F32)
    rel = (lax.broadcasted_iota(jnp.int32, (CHUNK, CHUNK), 0)
           - lax.broadcasted_iota(jnp.int32, (CHUNK, CHUNK), 1)).astype(_F32)
    rnw = rnw_ref[...]
    rnb = rnb_ref[...]
    for hd in range(N_HEADS):
        ret_dense[hd]()
        sl = slice(hd * D_HEAD, (hd + 1) * D_HEAD)
        lg = math.log(1.0 - 2.0 ** (-5.0 - hd))
        kf = rope(rk_s[:, sl]) * (D_HEAD ** -0.5)
        qb = rope(rq_s[:, sl]).astype(_BF16)
        kb = kf.astype(_BF16)
        kdec = (kf * jnp.exp(lg * (CHUNK - 1.0 - idx_r))).astype(_BF16)
        vh = rv_s[:, sl]
        dmat = jnp.where(rel >= 0, jnp.exp(lg * jnp.maximum(rel, 0.0)), 0.0)
        qdec = jnp.exp(lg * (idx_c + 1.0))
        states = [sret_ref[0, b, hd] for b in range(nb)]
        amats = [_dot_nt(qb[rs], kb[rs]) * dmat for rs in seq_rows]
        outs = [_dot(amats[b].astype(_BF16), vh[rs]) + _dot(qb[rs], states[b].astype(_BF16)) * qdec
                for b, rs in enumerate(seq_rows)]
        for b, rs in enumerate(seq_rows):
            sret_ref[0, b, hd] = math.exp(lg * CHUNK) * states[b] + _dot_tn(kdec[rs], vh[rs])
        o = jnp.concatenate(outs, axis=0)
        mu = jnp.mean(o, axis=-1, keepdims=True)
        oc = o - mu
        var = jnp.mean(oc * oc, axis=-1, keepdims=True)
        ob_s[:, sl] = oc * lax.rsqrt(var + GN_EPS) * rnw[:, sl] + rnb[:, sl]

    z_b = _dot((ob_s[...] * sgr_s[...]).astype(_BF16), wbr_ref[1])
    merged = m_s[...] + gb_s[...] * z_b
    y = x_ref[...].reshape(rows, dm) + _dot(merged.astype(_BF16), wout_ref[...])
    y_ref[...] = y.reshape(y_ref.shape)


def _layer_spec(block):
    zeros = (0,) * (len(block) - 1)
    return pl.BlockSpec(block, lambda t, l: (l[0],) + zeros, pipeline_mode=pl.Buffered(1))


def _const_spec(block):
    zeros = (0,) * len(block)
    return pl.BlockSpec(block, lambda t, l: zeros, pipeline_mode=pl.Buffered(1))


def _mixer_call(nt, lv, x, shg_in, sret_in, norm_w, w_in, lb_logits, hg_norm_w, ret_norm_w,
                ret_norm_b, w_branch, w_out, inv2, tril3, e8):
    nb, tt, dm = x.shape
    depth = w_in.shape[0]
    rows = nb * CHUNK
    d_in = w_in.shape[2]
    st_block = (None, nb, N_HEADS, D_HEAD, D_HEAD)
    x_spec = pl.BlockSpec((nb, CHUNK, dm), lambda t, l: (0, t, 0))
    in_specs = [
        x_spec,
        _layer_spec(st_block),
        _layer_spec(st_block),
        _layer_spec((None, 1, dm)),
        _layer_spec((None, dm, d_in)),
        _const_spec((depth, D_BRANCH)),
        _layer_spec((None, 1, D_HEAD)),
        _layer_spec((None, 1, D_BRANCH)),
        _layer_spec((None, 1, D_BRANCH)),
        _layer_spec((None, 2, D_BRANCH, dm)),
        _layer_spec((None, dm, dm)),
        _const_spec((1, D_HEAD)),
        _const_spec((CHUNK, 3 * CHUNK)),
        _const_spec((SUB * D_HEAD, D_HEAD)),
    ]
    st_out = jax.ShapeDtypeStruct((2, nb, N_HEADS, D_HEAD, D_HEAD), _F32)
    st_out_spec = pl.BlockSpec((1, nb, N_HEADS, D_HEAD, D_HEAD), lambda t, l: (t // nt, 0, 0, 0, 0))
    scratch = [
        pltpu.VMEM((rows, dm), _BF16),
        pltpu.VMEM((rows, D_BRANCH), _F32),
        pltpu.VMEM((rows, D_BRANCH), _F32),
        pltpu.VMEM((rows, D_BRANCH), _F32),
        pltpu.VMEM((rows, D_BRANCH), _BF16),
        pltpu.VMEM((rows, D_BRANCH), _F32),
        pltpu.VMEM((rows, D_BRANCH), _F32),
        pltpu.VMEM((rows, D_BRANCH), _BF16),
        pltpu.VMEM((rows, D_BRANCH), _F32),
        pltpu.VMEM((rows, D_BRANCH), _F32),
        pltpu.VMEM((rows, D_BRANCH), _F32),
        pltpu.VMEM((rows, D_BRANCH), _F32),
        pltpu.VMEM((rows, dm), _F32),
        pltpu.VMEM((rows, dm), _F32),
        pltpu.VMEM((rows, dm), _F32),
        pltpu.VMEM((N_HEADS, rows, N_SLOT * D_HEAD), _BF16),
        pltpu.VMEM((N_HEADS, rows, N_SLOT * D_HEAD), _BF16),
        pltpu.VMEM((2, rows, SUB * D_HEAD), _BF16),
    ]
    grid_spec = pltpu.PrefetchScalarGridSpec(
        num_scalar_prefetch=1, grid=(nt + 1,), in_specs=in_specs,
        out_specs=[x_spec, st_out_spec, st_out_spec], scratch_shapes=scratch)
    return pl.pallas_call(
        functools.partial(_mixer_kernel, nt),
        grid_spec=grid_spec,
        out_shape=[jax.ShapeDtypeStruct(x.shape, _F32), st_out, st_out],
        input_output_aliases={1: 0},
        compiler_params=pltpu.CompilerParams(dimension_semantics=("arbitrary",),
                                             vmem_limit_bytes=VMEM_LIMIT_BYTES),
        name="mixer",
    )(lv, x, shg_in, sret_in, norm_w, w_in, lb_logits, hg_norm_w, ret_norm_w, ret_norm_b, w_branch,
      w_out, inv2, tril3, e8)


def _ffn_kernel(nt, depth, l_ref, x_ref, cc_in_ref, nw_ref, wa_ref, wg_ref, cw_ref, cb_ref, wd_ref, nfw_ref,
                y_ref, cc_ref, h_s, acc_s, ext_s, g_s):
    t = pl.program_id(0)
    layer = l_ref[0]
    nb = x_ref.shape[0]
    rows = nb * CHUNK
    dm = x_ref.shape[2]
    nblk = wa_ref.shape[0]
    fb = wa_ref.shape[2]

    @pl.when(t == 0)
    def _():
        cc_ref[...] = jnp.zeros(cc_ref.shape, _F32)

    @pl.when(t == nt)
    def _():
        cc_ref[0] = cc_in_ref[...]

    x = x_ref[...].reshape(rows, dm)
    h_s[...] = _rms_rows(x, nw_ref[...]).astype(_BF16)

    def up(i):
        h = h_s[...]
        ext_s[i % 2, :, 8:8 + CHUNK, :] = _dot(h, wa_ref[i]).reshape(nb, CHUNK, fb)
        g_s[i % 2] = _dot(h, wg_ref[i])

    def down(i):
        ext = ext_s.at[i % 2]
        ext[:, 6:8, :] = cc_ref[0, i]
        a3 = ext[:, 8:8 + CHUNK, :]
        cw = cw_ref[i]
        conv = (cb_ref[i] + ext[:, 6:6 + CHUNK, :] * cw[0:1, :] + ext[:, 7:7 + CHUNK, :] * cw[1:2, :]
                + a3 * cw[2:3, :])
        cc_ref[0, i] = a3[:, CHUNK - 2:CHUNK, :]
        u = (_silu(conv).reshape(rows, fb) * g_s[i % 2]).astype(_BF16)
        part = _dot(u, wd_ref[i])
        if i == 0:
            acc_s[...] = part
        else:
            acc_s[...] += part

    up(0)
    for i in range(nblk):
        if i + 1 < nblk:
            up(i + 1)
        down(i)

    y = x_ref[...].reshape(rows, dm) + acc_s[...]

    @pl.when(layer == depth - 1)
    def _():
        y_ref[...] = _rms_rows(y, nfw_ref[...]).reshape(y_ref.shape)

    @pl.when(layer != depth - 1)
    def _():
        y_ref[...] = y.reshape(y_ref.shape)


def _ffn_call(nt, lv, x, cc_in, norm_w, wa, wg, cw, cb, wd, nfw):
    nb, tt, dm = x.shape
    rows = nb * CHUNK
    depth, nblk, fb = wa.shape[0], wa.shape[1], wa.shape[3]
    x_spec = pl.BlockSpec((nb, CHUNK, dm), lambda t, l: (0, t, 0))
    in_specs = [
        x_spec,
        _layer_spec((None, nblk, nb, 2, fb)),
        _layer_spec((None, 1, dm)),
        _layer_spec((None, nblk, dm, fb)),
        _layer_spec((None, nblk, dm, fb)),
        _layer_spec((None, nblk, 3, fb)),
        _layer_spec((None, nblk, 1, fb)),
        _layer_spec((None, nblk, fb, dm)),
        _const_spec((1, dm)),
    ]
    cc_out = jax.ShapeDtypeStruct((2, nblk, nb, 2, fb), _F32)
    cc_spec = pl.BlockSpec((1, nblk, nb, 2, fb), lambda t, l: (t // nt, 0, 0, 0, 0))
    scratch = [
        pltpu.VMEM((rows, dm), _BF16),
        pltpu.VMEM((rows, dm), _F32),
        pltpu.VMEM((2, nb, CHUNK + 8, fb), _F32),
        pltpu.VMEM((2, rows, fb), _F32),
    ]
    grid_spec = pltpu.PrefetchScalarGridSpec(
        num_scalar_prefetch=1, grid=(nt + 1,), in_specs=in_specs,
        out_specs=[x_spec, cc_spec], scratch_shapes=scratch)
    return pl.pallas_call(
        functools.partial(_ffn_kernel, nt, depth),
        grid_spec=grid_spec,
        out_shape=[jax.ShapeDtypeStruct(x.shape, _F32), cc_out],
        input_output_aliases={1: 0},
        compiler_params=pltpu.CompilerParams(dimension_semantics=("arbitrary",),
                                             vmem_limit_bytes=VMEM_LIMIT_BYTES),
        name="ffn",
    )(lv, x, cc_in, norm_w, wa, wg, cw, cb, wd, nfw)


def _constants():
    half = D_HEAD // 2
    inv = ROPE_BASE ** (-jnp.arange(half, dtype=_F32) / half)
    inv2 = jnp.concatenate([inv, inv])[None, :]
    r = np.arange(CHUNK)
    tril = (r[:, None] >= r[None, :]).astype(np.float32)
    tril3 = jnp.asarray(np.concatenate([tril, tril, tril], axis=1), _BF16)
    e8 = np.zeros((SUB * D_HEAD, D_HEAD), np.float32)
    cols = np.arange(CHUNK)
    for s in range(SUB):
        e8[s * D_HEAD:(s + 1) * D_HEAD, cols[cols % SUB == s]] = 1.0
    return inv2, tril3, jnp.asarray(e8, _BF16)


def kernel(x_prompt, x_sample, state_hgrn, state_ret, cache_conv, norm_mix_w, w_in, hg_lb_logits,
           hg_norm_w, ret_norm_w, ret_norm_b, w_branch, w_out, norm_ffn_w, w_up, conv_w, conv_b,
           w_down, norm_final_w):
    nb, tp, dm = x_prompt.shape
    depth = w_in.shape[0]
    d_ff = w_down.shape[1]
    assert x_sample.shape == (nb, CHUNK, dm) and tp % CHUNK == 0 and d_ff % FFN_BLOCK == 0
    nt = tp // CHUNK
    nblk = d_ff // FFN_BLOCK
    inv2, tril3, e8 = _constants()

    w_in_b = w_in.astype(_BF16)
    w_branch_b = w_branch.astype(_BF16)
    w_out_b = w_out.astype(_BF16)
    w_up_b = w_up.astype(_BF16).reshape(depth, dm, 2, nblk, FFN_BLOCK)
    wa = jnp.transpose(w_up_b[:, :, 0], (0, 2, 1, 3))
    wg = jnp.transpose(w_up_b[:, :, 1], (0, 2, 1, 3))
    wd = w_down.astype(_BF16).reshape(depth, nblk, FFN_BLOCK, dm)
    cw = jnp.transpose(conv_w.reshape(depth, 3, nblk, FFN_BLOCK), (0, 2, 1, 3))
    cb = conv_b.reshape(depth, nblk, 1, FFN_BLOCK)
    cc_in = jnp.transpose(cache_conv.reshape(depth, nb, 2, nblk, FFN_BLOCK), (0, 3, 1, 2, 4))
    shg_in = jnp.swapaxes(state_hgrn, -1, -2)
    nmw = norm_mix_w[:, None, :]
    nfw_l = norm_ffn_w[:, None, :]
    hgw = hg_norm_w[:, None, :]
    rnw = ret_norm_w[:, None, :]
    rnb = ret_norm_b[:, None, :]
    nfin = norm_final_w[None, :]

    def layer_fn(x, l):
        lv = jnp.reshape(l, (1,))
        x, s_hg, s_ret = _mixer_call(nt, lv, x, shg_in, state_ret, nmw, w_in_b, hg_lb_logits, hgw, rnw, rnb,
                                     w_branch_b, w_out_b, inv2, tril3, e8)
        x, s_cc = _ffn_call(nt, lv, x, cc_in, nfw_l, wa, wg, cw, cb, wd, nfin)
        return x, (s_hg, s_ret, s_cc)

    x = jnp.concatenate([x_prompt, x_sample], axis=1)
    x, (hg_all, ret_all, cc_all) = lax.scan(layer_fn, x, jnp.arange(depth, dtype=jnp.int32))
    hg_all = jnp.swapaxes(hg_all, -1, -2)
    cc_all = jnp.transpose(cc_all, (0, 1, 3, 4, 2, 5)).reshape(depth, 2, nb, 2, d_ff)
    return (x[:, :tp], x[:, tp:], hg_all[:, 0], ret_all[:, 0], cc_all[:, 0],
            hg_all[:, 1], ret_all[:, 1], cc_all[:, 1])
```

```python
import functools
import math

import numpy as np
import jax
import jax.numpy as jnp
from jax import lax
from jax.experimental import pallas as pl
from jax.experimental.pallas import tpu as pltpu

CHUNK = 64
SUB = 8
LEVELS = (64, 32, 16)
N_SLOT = sum(CHUNK // m for m in LEVELS)
N_HEADS = 4
D_HEAD = 128
D_BRANCH = N_HEADS * D_HEAD
PAST_LEN = 2048
ROPE_BASE = 10000.0
NORM_EPS = 1e-6
GN_EPS = 1e-5
F_FLOOR = 1e-30
MID = CHUNK // 2
MID_SPREAD_MAX = 80.0
FFN_BLOCK = 256
VMEM_LIMIT_BYTES = 60 * 1024 * 1024

_F32 = jnp.float32
_BF16 = jnp.bfloat16


def _dot(a, b):
    return jnp.dot(a, b, preferred_element_type=_F32)


def _dot_nt(a, b):
    return lax.dot_general(a, b, (((1,), (1,)), ((), ())), preferred_element_type=_F32)


def _dot_tn(a, b):
    return lax.dot_general(a, b, (((0,), (0,)), ((), ())), preferred_element_type=_F32)


def _sigmoid(x):
    return jax.nn.sigmoid(x)


def _silu(x):
    return x * jax.nn.sigmoid(x)


def _rms_rows(x, w):
    ms = jnp.mean(x * x, axis=-1, keepdims=True)
    return x * lax.rsqrt(ms + NORM_EPS) * w


def _factored_scores(qh, kh, bh, cat1, cat2, p_buf, e8_ref):
    rows = qh.shape[0]
    jrow = lax.broadcasted_iota(jnp.int32, (rows, D_HEAD), 0)
    slot = 0
    for m in LEVELS:
        half = m // 2
        bm = bh.reshape(rows // m, m, D_HEAD)
        delta = bm - bm[:, half - 1:half, :]
        is_q = lax.broadcasted_iota(jnp.int32, bm.shape, 1) >= half
        dec = jnp.exp(jnp.minimum(jnp.where(is_q, delta, -delta), 0.0))
        xm = (jnp.where(is_q, qh.reshape(bm.shape), kh.reshape(bm.shape)) * dec).reshape(rows, D_HEAD)
        hidx = jrow >> int(math.log2(half))
        for s in range(rows // m):
            cs = slice(slot * D_HEAD, (slot + 1) * D_HEAD)
            cat1[:, cs] = jnp.where(hidx == 2 * s + 1, xm, 0.0).astype(_BF16)
            cat2[:, cs] = jnp.where(hidx == 2 * s, xm, 0.0).astype(_BF16)
            slot += 1
    rowg = lax.broadcasted_iota(jnp.int32, (rows // SUB, SUB, D_HEAD), 1)
    qg = qh.reshape(rowg.shape)
    kg = kh.reshape(rowg.shape)
    bg = bh.reshape(rowg.shape)
    for s in range(SUB):
        msk = rowg >= s
        dec = jnp.exp(jnp.where(msk, bg - bg[:, s:s + 1, :], 0.0))
        ps = jnp.where(msk, dec * qg * kg[:, s:s + 1, :], 0.0)
        p_buf[:, s * D_HEAD:(s + 1) * D_HEAD] = ps.reshape(rows, D_HEAD).astype(_BF16)
    ad_col = lax.broadcasted_iota(jnp.int32, (rows, D_HEAD), 1)
    ad = jnp.where((ad_col >> 3) == (jrow >> 3), _dot(p_buf[...], e8_ref[...]), 0.0)
    return _dot_nt(cat1[...], cat2[...]) + ad[:, 0:CHUNK]


def _mixer_kernel(nt, l_ref, x_ref, shg_in_ref, sret_in_ref, nw_ref, win_ref, lbl_ref, hgw_ref,
                  rnw_ref, rnb_ref, wbr_ref, wout_ref, inv_ref, tril_ref, e8_ref,
                  y_ref, shg_ref, sret_ref,
                  h_s, q_s, k_s, bc_s, v_s, rq_s, rk_s, rv_s, sga_s, sgr_s, oa_s, ob_s, za_s, m_s, gb_s,
                  cat1_s, cat2_s, p_s, am_s):
    t = pl.program_id(0)
    layer = l_ref[0]
    nb = x_ref.shape[0]
    rows = nb * CHUNK
    dm = x_ref.shape[2]
    seq_rows = [slice(b * CHUNK, (b + 1) * CHUNK) for b in range(nb)]

    @pl.when(t == 0)
    def _():
        shg_ref[...] = jnp.zeros(shg_ref.shape, _F32)
        sret_ref[...] = jnp.zeros(sret_ref.shape, _F32)

    @pl.when(t == nt)
    def _():
        shg_ref[0] = shg_in_ref[...]
        sret_ref[0] = sret_in_ref[...]

    x = x_ref[...].reshape(rows, dm)
    h_s[...] = _rms_rows(x, nw_ref[...]).astype(_BF16)

    def proj(group, width=D_BRANCH, offset=0):
        c0 = group * D_BRANCH + offset
        return _dot(h_s[...], win_ref[:, c0:c0 + width])

    logits = lbl_ref[...]
    ex = jnp.exp(logits - jnp.max(logits, axis=0, keepdims=True))
    prob = ex / jnp.sum(ex, axis=0, keepdims=True)
    lrow = lax.broadcasted_iota(jnp.int32, prob.shape, 0)
    lb = jnp.sum(jnp.where((lrow >= 1) & (lrow <= layer), prob, 0.0), axis=0, keepdims=True)

    q_s[...] = _silu(proj(0))
    f_gate = lb + (1.0 - lb) * _sigmoid(proj(1))
    k_s[...] = 1.0 - f_gate
    log_f = jnp.log(jnp.maximum(f_gate, F_FLOOR))
    lf_hi = log_f.astype(_BF16)
    r1 = log_f - lf_hi.astype(_F32)
    lf_mid = r1.astype(_BF16)
    lf_lo = (r1 - lf_mid.astype(_F32)).astype(_BF16)
    tril3 = tril_ref[...]
    for rs in seq_rows:
        bc_s[rs, :] = _dot(tril3, jnp.concatenate([lf_hi[rs], lf_mid[rs], lf_lo[rs]], axis=0))
    v_s[...] = proj(2).astype(_BF16)

    jrow = lax.broadcasted_iota(jnp.int32, (rows, D_HEAD), 0) & (CHUNK - 1)
    hgw = hgw_ref[...]

    ball = bc_s[...].reshape(nb, CHUNK, D_BRANCH)
    spread = jnp.max(jnp.abs(ball - ball[:, MID - 1:MID, :]))
    mid_ok = spread <= MID_SPREAD_MAX

    @pl.when(t == 0)
    def _():
        am_s[...] = jnp.zeros(am_s.shape, _F32)

    @pl.when(jnp.logical_not(mid_ok))
    def _():
        def seq_body(b, carry):
            rs = pl.ds(pl.multiple_of(b * CHUNK, CHUNK), CHUNK)
            for hd in range(N_HEADS):
                sl = slice(hd * D_HEAD, (hd + 1) * D_HEAD)
                am_s[hd, rs, :] = _factored_scores(q_s[rs, sl], k_s[rs, sl], bc_s[rs, sl], cat1_s.at[hd],
                                                   cat2_s.at[hd], p_s.at[hd], e8_ref)
            return carry

        lax.fori_loop(0, nb, seq_body, 0)

    causal = (lax.broadcasted_iota(jnp.int32, (CHUNK, CHUNK), 0)
              >= lax.broadcasted_iota(jnp.int32, (CHUNK, CHUNK), 1))

    def dense_rq():
        rq_s[...] = proj(4)

    def dense_rk():
        rk_s[...] = proj(5)

    def dense_rv():
        rv_s[...] = proj(6).astype(_BF16)

    def dense_hgate():
        sga_s[...] = _silu(proj(3))

    hg_dense = (dense_rq, dense_rk, dense_rv, dense_hgate)

    for hd in range(N_HEADS):
        hg_dense[hd]()
        sl = slice(hd * D_HEAD, (hd + 1) * D_HEAD)
        qh = q_s[:, sl]
        kh = k_s[:, sl]
        bh = bc_s[:, sl]
        vh = v_s[:, sl]
        b3 = bh.reshape(nb, CHUNK, D_HEAD)
        q3 = qh.reshape(b3.shape)
        k3 = kh.reshape(b3.shape)
        blast = b3[:, CHUNK - 1:CHUNK, :]
        bmid = b3[:, MID - 1:MID, :]
        qe = (qh * jnp.exp(bh)).astype(_BF16)
        kd = (k3 * jnp.exp(blast - b3)).reshape(rows, D_HEAD).astype(_BF16)
        qm = (q3 * jnp.exp(b3 - bmid)).reshape(rows, D_HEAD).astype(_BF16)
        km = (k3 * jnp.exp(bmid - b3)).reshape(rows, D_HEAD).astype(_BF16)
        chunk_dec = jnp.exp(blast)
        states = [shg_ref[0, b, hd] for b in range(nb)]
        amats = [jnp.where(mid_ok, jnp.where(causal, _dot_nt(qm[rs], km[rs]), 0.0), am_s[hd, rs, :])
                 for rs in seq_rows]
        outs = [_dot(amats[b].astype(_BF16), vh[rs]) + _dot_nt(qe[rs], states[b].astype(_BF16))
                for b, rs in enumerate(seq_rows)]
        for b, rs in enumerate(seq_rows):
            shg_ref[0, b, hd] = states[b] * chunk_dec[b] + _dot_tn(vh[rs], kd[rs])
        oa_s[:, sl] = _rms_rows(jnp.concatenate(outs, axis=0), hgw)

    pos0 = jnp.where(t == nt, PAST_LEN, t * CHUNK)
    posf = (lax.broadcasted_iota(jnp.int32, (CHUNK, D_HEAD), 0) + pos0).astype(_F32)
    ang = posf * inv_ref[...]
    lane = lax.broadcasted_iota(jnp.int32, (CHUNK, D_HEAD), 1)
    cosf = jnp.cos(ang)
    sinf = jnp.where(lane < D_HEAD // 2, -1.0, 1.0) * jnp.sin(ang)
    cos3 = jnp.broadcast_to(cosf[None], (nb, CHUNK, D_HEAD)).reshape(rows, D_HEAD)
    sin3 = jnp.broadcast_to(sinf[None], (nb, CHUNK, D_HEAD)).reshape(rows, D_HEAD)

    def rope(v):
        return v * cos3 + pltpu.roll(v, D_HEAD // 2, 1) * sin3

    def dense_rgate_za():
        sgr_s[...] = _silu(proj(7))
        za_s[...] = _dot((oa_s[...] * sga_s[...]).astype(_BF16), wbr_ref[0])

    def dense_gate_a():
        m_s[...] = _sigmoid(proj(8, dm)) * za_s[...]

    def dense_gate_b0():
        gb_s[:, 0:dm // 2] = _sigmoid(proj(8, dm // 2, dm))

    def dense_gate_b1():
        gb_s[:, dm // 2:dm] = _sigmoid(proj(8, dm // 2, dm + dm // 2))

    ret_dense = (dense_rgate_za, dense_gate_a, dense_gate_b0, dense_gate_b1)

    idx_r = jrow.astype(_F32)
    idx_c = lax.broadcasted_iota(jnp.int32, (CHUNK, D_HEAD), 0).astype(_F32)
    rel = (lax.broadcasted_iota(jnp.int32, (CHUNK, CHUNK), 0)
           - lax.broadcasted_iota(jnp.int32, (CHUNK, CHUNK), 1)).astype(_F32)
    rnw = rnw_ref[...]
    rnb = rnb_ref[...]
    for hd in range(N_HEADS):
        ret_dense[hd]()
        sl = slice(hd * D_HEAD, (hd + 1) * D_HEAD)
        lg = math.log(1.0 - 2.0 ** (-5.0 - hd))
        kf = rope(rk_s[:, sl]) * (D_HEAD ** -0.5)
        qb = rope(rq_s[:, sl]).astype(_BF16)
        kb = kf.astype(_BF16)
        kdec = (kf * jnp.exp(lg * (CHUNK - 1.0 - idx_r))).astype(_BF16)
        vh = rv_s[:, sl]
        dmat = jnp.where(rel >= 0, jnp.exp(lg * jnp.maximum(rel, 0.0)), 0.0)
        qdec = jnp.exp(lg * (idx_c + 1.0))
        states = [sret_ref[0, b, hd] for b in range(nb)]
        amats = [_dot_nt(qb[rs], kb[rs]) * dmat for rs in seq_rows]
        outs = [_dot(amats[b].astype(_BF16), vh[rs]) + _dot(qb[rs], states[b].astype(_BF16)) * qdec
                for b, rs in enumerate(seq_rows)]
        for b, rs in enumerate(seq_rows):
            sret_ref[0, b, hd] = math.exp(lg * CHUNK) * states[b] + _dot_tn(kdec[rs], vh[rs])
        o = jnp.concatenate(outs, axis=0)
        mu = jnp.mean(o, axis=-1, keepdims=True)
        oc = o - mu
        var = jnp.mean(oc * oc, axis=-1, keepdims=True)
        ob_s[:, sl] = oc * lax.rsqrt(var + GN_EPS) * rnw[:, sl] + rnb[:, sl]

    z_b = _dot((ob_s[...] * sgr_s[...]).astype(_BF16), wbr_ref[1])
    merged = m_s[...] + gb_s[...] * z_b
    y = x_ref[...].reshape(rows, dm) + _dot(merged.astype(_BF16), wout_ref[...])
    y_ref[...] = y.reshape(y_ref.shape)


def _layer_spec(block):
    zeros = (0,) * (len(block) - 1)
    return pl.BlockSpec(block, lambda t, l: (l[0],) + zeros, pipeline_mode=pl.Buffered(1))


def _const_spec(block):
    zeros = (0,) * len(block)
    return pl.BlockSpec(block, lambda t, l: zeros, pipeline_mode=pl.Buffered(1))


def _mixer_call(nt, lv, x, shg_in, sret_in, norm_w, w_in, lb_logits, hg_norm_w, ret_norm_w,
                ret_norm_b, w_branch, w_out, inv2, tril3, e8):
    nb, tt, dm = x.shape
    depth = w_in.shape[0]
    rows = nb * CHUNK
    d_in = w_in.shape[2]
    st_block = (None, nb, N_HEADS, D_HEAD, D_HEAD)
    x_spec = pl.BlockSpec((nb, CHUNK, dm), lambda t, l: (0, t, 0))
    in_specs = [
        x_spec,
        _layer_spec(st_block),
        _layer_spec(st_block),
        _layer_spec((None, 1, dm)),
        _layer_spec((None, dm, d_in)),
        _const_spec((depth, D_BRANCH)),
        _layer_spec((None, 1, D_HEAD)),
        _layer_spec((None, 1, D_BRANCH)),
        _layer_spec((None, 1, D_BRANCH)),
        _layer_spec((None, 2, D_BRANCH, dm)),
        _layer_spec((None, dm, dm)),
        _const_spec((1, D_HEAD)),
        _const_spec((CHUNK, 3 * CHUNK)),
        _const_spec((SUB * D_HEAD, D_HEAD)),
    ]
    st_out = jax.ShapeDtypeStruct((2, nb, N_HEADS, D_HEAD, D_HEAD), _F32)
    st_out_spec = pl.BlockSpec((1, nb, N_HEADS, D_HEAD, D_HEAD), lambda t, l: (t // nt, 0, 0, 0, 0))
    scratch = [
        pltpu.VMEM((rows, dm), _BF16),
        pltpu.VMEM((rows, D_BRANCH), _F32),
        pltpu.VMEM((rows, D_BRANCH), _F32),
        pltpu.VMEM((rows, D_BRANCH), _F32),
        pltpu.VMEM((rows, D_BRANCH), _BF16),
        pltpu.VMEM((rows, D_BRANCH), _F32),
        pltpu.VMEM((rows, D_BRANCH), _F32),
        pltpu.VMEM((rows, D_BRANCH), _BF16),
        pltpu.VMEM((rows, D_BRANCH), _F32),
        pltpu.VMEM((rows, D_BRANCH), _F32),
        pltpu.VMEM((rows, D_BRANCH), _F32),
        pltpu.VMEM((rows, D_BRANCH), _F32),
        pltpu.VMEM((rows, dm), _F32),
        pltpu.VMEM((rows, dm), _F32),
        pltpu.VMEM((rows, dm), _F32),
        pltpu.VMEM((N_HEADS, CHUNK, N_SLOT * D_HEAD), _BF16),
        pltpu.VMEM((N_HEADS, CHUNK, N_SLOT * D_HEAD), _BF16),
        pltpu.VMEM((N_HEADS, CHUNK, SUB * D_HEAD), _BF16),
        pltpu.VMEM((N_HEADS, rows, CHUNK), _F32),
    ]
    grid_spec = pltpu.PrefetchScalarGridSpec(
        num_scalar_prefetch=1, grid=(nt + 1,), in_specs=in_specs,
        out_specs=[x_spec, st_out_spec, st_out_spec], scratch_shapes=scratch)
    return pl.pallas_call(
        functools.partial(_mixer_kernel, nt),
        grid_spec=grid_spec,
        out_shape=[jax.ShapeDtypeStruct(x.shape, _F32), st_out, st_out],
        input_output_aliases={1: 0},
        compiler_params=pltpu.CompilerParams(dimension_semantics=("arbitrary",),
                                             vmem_limit_bytes=VMEM_LIMIT_BYTES),
        name="mixer",
    )(lv, x, shg_in, sret_in, norm_w, w_in, lb_logits, hg_norm_w, ret_norm_w, ret_norm_b, w_branch,
      w_out, inv2, tril3, e8)


def _ffn_kernel(nt, depth, l_ref, x_ref, cc_in_ref, nw_ref, wa_ref, wg_ref, cw_ref, cb_ref, wd_ref, nfw_ref,
                y_ref, cc_ref, h_s, acc_s, ext_s, g_s):
    t = pl.program_id(0)
    layer = l_ref[0]
    nb = x_ref.shape[0]
    rows = nb * CHUNK
    dm = x_ref.shape[2]
    nblk = wa_ref.shape[0]
    fb = wa_ref.shape[2]

    @pl.when(t == 0)
    def _():
        cc_ref[...] = jnp.zeros(cc_ref.shape, _F32)

    @pl.when(t == nt)
    def _():
        cc_ref[0] = cc_in_ref[...]

    x = x_ref[...].reshape(rows, dm)
    h_s[...] = _rms_rows(x, nw_ref[...]).astype(_BF16)

    def up(i):
        h = h_s[...]
        ext_s[i % 2, :, 8:8 + CHUNK, :] = _dot(h, wa_ref[i]).reshape(nb, CHUNK, fb)
        g_s[i % 2] = _dot(h, wg_ref[i])

    def down(i):
        ext = ext_s.at[i % 2]
        ext[:, 6:8, :] = cc_ref[0, i]
        a3 = ext[:, 8:8 + CHUNK, :]
        cw = cw_ref[i]
        conv = (cb_ref[i] + ext[:, 6:6 + CHUNK, :] * cw[0:1, :] + ext[:, 7:7 + CHUNK, :] * cw[1:2, :]
                + a3 * cw[2:3, :])
        cc_ref[0, i] = a3[:, CHUNK - 2:CHUNK, :]
        u = (_silu(conv).reshape(rows, fb) * g_s[i % 2]).astype(_BF16)
        part = _dot(u, wd_ref[i])
        if i == 0:
            acc_s[...] = part
        else:
            acc_s[...] += part

    up(0)
    for i in range(nblk):
        if i + 1 < nblk:
            up(i + 1)
        down(i)

    y = x_ref[...].reshape(rows, dm) + acc_s[...]

    @pl.when(layer == depth - 1)
    def _():
        y_ref[...] = _rms_rows(y, nfw_ref[...]).reshape(y_ref.shape)

    @pl.when(layer != depth - 1)
    def _():
        y_ref[...] = y.reshape(y_ref.shape)


def _ffn_call(nt, lv, x, cc_in, norm_w, wa, wg, cw, cb, wd, nfw):
    nb, tt, dm = x.shape
    rows = nb * CHUNK
    depth, nblk, fb = wa.shape[0], wa.shape[1], wa.shape[3]
    x_spec = pl.BlockSpec((nb, CHUNK, dm), lambda t, l: (0, t, 0))
    in_specs = [
        x_spec,
        _layer_spec((None, nblk, nb, 2, fb)),
        _layer_spec((None, 1, dm)),
        _layer_spec((None, nblk, dm, fb)),
        _layer_spec((None, nblk, dm, fb)),
        _layer_spec((None, nblk, 3, fb)),
        _layer_spec((None, nblk, 1, fb)),
        _layer_spec((None, nblk, fb, dm)),
        _const_spec((1, dm)),
    ]
    cc_out = jax.ShapeDtypeStruct((2, nblk, nb, 2, fb), _F32)
    cc_spec = pl.BlockSpec((1, nblk, nb, 2, fb), lambda t, l: (t // nt, 0, 0, 0, 0))
    scratch = [
        pltpu.VMEM((rows, dm), _BF16),
        pltpu.VMEM((rows, dm), _F32),
        pltpu.VMEM((2, nb, CHUNK + 8, fb), _F32),
        pltpu.VMEM((2, rows, fb), _F32),
    ]
    grid_spec = pltpu.PrefetchScalarGridSpec(
        num_scalar_prefetch=1, grid=(nt + 1,), in_specs=in_specs,
        out_specs=[x_spec, cc_spec], scratch_shapes=scratch)
    return pl.pallas_call(
        functools.partial(_ffn_kernel, nt, depth),
        grid_spec=grid_spec,
        out_shape=[jax.ShapeDtypeStruct(x.shape, _F32), cc_out],
        input_output_aliases={1: 0},
        compiler_params=pltpu.CompilerParams(dimension_semantics=("arbitrary",),
                                             vmem_limit_bytes=VMEM_LIMIT_BYTES),
        name="ffn",
    )(lv, x, cc_in, norm_w, wa, wg, cw, cb, wd, nfw)


def _constants():
    half = D_HEAD // 2
    inv = ROPE_BASE ** (-jnp.arange(half, dtype=_F32) / half)
    inv2 = jnp.concatenate([inv, inv])[None, :]
    r = np.arange(CHUNK)
    tril = (r[:, None] >= r[None, :]).astype(np.float32)
    tril3 = jnp.asarray(np.concatenate([tril, tril, tril], axis=1), _BF16)
    e8 = np.zeros((SUB * D_HEAD, D_HEAD), np.float32)
    cols = np.arange(CHUNK)
    for s in range(SUB):
        e8[s * D_HEAD:(s + 1) * D_HEAD, cols[cols % SUB == s]] = 1.0
    return inv2, tril3, jnp.asarray(e8, _BF16)


def kernel(x_prompt, x_sample, state_hgrn, state_ret, cache_conv, norm_mix_w, w_in, hg_lb_logits,
           hg_norm_w, ret_norm_w, ret_norm_b, w_branch, w_out, norm_ffn_w, w_up, conv_w, conv_b,
           w_down, norm_final_w):
    nb, tp, dm = x_prompt.shape
    depth = w_in.shape[0]
    d_ff = w_down.shape[1]
    assert x_sample.shape == (nb, CHUNK, dm) and tp % CHUNK == 0 and d_ff % FFN_BLOCK == 0
    nt = tp // CHUNK
    nblk = d_ff // FFN_BLOCK
    inv2, tril3, e8 = _constants()

    w_in_b = w_in.astype(_BF16)
    w_branch_b = w_branch.astype(_BF16)
    w_out_b = w_out.astype(_BF16)
    w_up_b = w_up.astype(_BF16).reshape(depth, dm, 2, nblk, FFN_BLOCK)
    wa = jnp.transpose(w_up_b[:, :, 0], (0, 2, 1, 3))
    wg = jnp.transpose(w_up_b[:, :, 1], (0, 2, 1, 3))
    wd = w_down.astype(_BF16).reshape(depth, nblk, FFN_BLOCK, dm)
    cw = jnp.transpose(conv_w.reshape(depth, 3, nblk, FFN_BLOCK), (0, 2, 1, 3))
    cb = conv_b.reshape(depth, nblk, 1, FFN_BLOCK)
    cc_in = jnp.transpose(cache_conv.reshape(depth, nb, 2, nblk, FFN_BLOCK), (0, 3, 1, 2, 4))
    shg_in = jnp.swapaxes(state_hgrn, -1, -2)
    nmw = norm_mix_w[:, None, :]
    nfw_l = norm_ffn_w[:, None, :]
    hgw = hg_norm_w[:, None, :]
    rnw = ret_norm_w[:, None, :]
    rnb = ret_norm_b[:, None, :]
    nfin = norm_final_w[None, :]

    def layer_fn(x, l):
        lv = jnp.reshape(l, (1,))
        x, s_hg, s_ret = _mixer_call(nt, lv, x, shg_in, state_ret, nmw, w_in_b, hg_lb_logits, hgw, rnw, rnb,
                                     w_branch_b, w_out_b, inv2, tril3, e8)
        x, s_cc = _ffn_call(nt, lv, x, cc_in, nfw_l, wa, wg, cw, cb, wd, nfin)
        return x, (s_hg, s_ret, s_cc)

    x = jnp.concatenate([x_prompt, x_sample], axis=1)
    x, (hg_all, ret_all, cc_all) = lax.scan(layer_fn, x, jnp.arange(depth, dtype=jnp.int32))
    hg_all = jnp.swapaxes(hg_all, -1, -2)
    cc_all = jnp.transpose(cc_all, (0, 1, 3, 4, 2, 5)).reshape(depth, 2, nb, 2, d_ff)
    return (x[:, :tp], x[:, tp:], hg_all[:, 0], ret_all[:, 0], cc_all[:, 0],
            hg_all[:, 1], ret_all[:, 1], cc_all[:, 1])
```

```python
import functools
import math

import numpy as np
import jax
import jax.numpy as jnp
from jax import lax
from jax.experimental import pallas as pl
from jax.experimental.pallas import tpu as pltpu

CHUNK = 64
SUB = 8
LEVELS = (64, 32, 16)
N_SLOT = sum(CHUNK // m for m in LEVELS)
N_HEADS = 4
D_HEAD = 128
D_BRANCH = N_HEADS * D_HEAD
PAST_LEN = 2048
ROPE_BASE = 10000.0
NORM_EPS = 1e-6
GN_EPS = 1e-5
F_FLOOR = 1e-30
MID = CHUNK // 2
MID_SPREAD_MAX = 80.0
FFN_BLOCK = 256
VMEM_LIMIT_BYTES = 60 * 1024 * 1024

_F32 = jnp.float32
_BF16 = jnp.bfloat16


def _dot(a, b):
    return jnp.dot(a, b, preferred_element_type=_F32)


def _dot_nt(a, b):
    return lax.dot_general(a, b, (((1,), (1,)), ((), ())), preferred_element_type=_F32)


def _dot_tn(a, b):
    return lax.dot_general(a, b, (((0,), (0,)), ((), ())), preferred_element_type=_F32)


def _sigmoid(x):
    return jax.nn.sigmoid(x)


def _silu(x):
    return x * jax.nn.sigmoid(x)


def _rms_rows(x, w):
    ms = jnp.mean(x * x, axis=-1, keepdims=True)
    return x * lax.rsqrt(ms + NORM_EPS) * w


def _factored_scores(qh, kh, bh, cat1, cat2, p_buf, e8_ref):
    rows = qh.shape[0]
    jrow = lax.broadcasted_iota(jnp.int32, (rows, D_HEAD), 0)
    slot = 0
    for m in LEVELS:
        half = m // 2
        bm = bh.reshape(rows // m, m, D_HEAD)
        delta = bm - bm[:, half - 1:half, :]
        is_q = lax.broadcasted_iota(jnp.int32, bm.shape, 1) >= half
        dec = jnp.exp(jnp.minimum(jnp.where(is_q, delta, -delta), 0.0))
        xm = (jnp.where(is_q, qh.reshape(bm.shape), kh.reshape(bm.shape)) * dec).reshape(rows, D_HEAD)
        hidx = jrow >> int(math.log2(half))
        for s in range(rows // m):
            cs = slice(slot * D_HEAD, (slot + 1) * D_HEAD)
            cat1[:, cs] = jnp.where(hidx == 2 * s + 1, xm, 0.0).astype(_BF16)
            cat2[:, cs] = jnp.where(hidx == 2 * s, xm, 0.0).astype(_BF16)
            slot += 1
    rowg = lax.broadcasted_iota(jnp.int32, (rows // SUB, SUB, D_HEAD), 1)
    qg = qh.reshape(rowg.shape)
    kg = kh.reshape(rowg.shape)
    bg = bh.reshape(rowg.shape)
    for s in range(SUB):
        msk = rowg >= s
        dec = jnp.exp(jnp.where(msk, bg - bg[:, s:s + 1, :], 0.0))
        ps = jnp.where(msk, dec * qg * kg[:, s:s + 1, :], 0.0)
        p_buf[:, s * D_HEAD:(s + 1) * D_HEAD] = ps.reshape(rows, D_HEAD).astype(_BF16)
    ad_col = lax.broadcasted_iota(jnp.int32, (rows, D_HEAD), 1)
    ad = jnp.where((ad_col >> 3) == (jrow >> 3), _dot(p_buf[...], e8_ref[...]), 0.0)
    return _dot_nt(cat1[...], cat2[...]) + ad[:, 0:CHUNK]


def _mixer_kernel(nt, first, l_ref, *refs):
    if first:
        xp_ref, xs_ref, *refs = refs
    else:
        x_ref, *refs = refs
    (shg_in_ref, sret_in_ref, nw_ref, win_ref, lbl_ref, hgw_ref, rnw_ref, rnb_ref, wbr_ref, wout_ref, inv_ref,
     tril_ref, e8_ref,
     y_ref, shg_ref, sret_ref,
     h_s, q_s, k_s, bc_s, v_s, rq_s, rk_s, rv_s, sga_s, sgr_s, oa_s, ob_s, za_s, m_s, gb_s,
     cat1_s, cat2_s, p_s, am_s) = refs
    t = pl.program_id(0)
    layer = l_ref[0]
    nb, _, dm = y_ref.shape
    rows = nb * CHUNK
    seq_rows = [slice(b * CHUNK, (b + 1) * CHUNK) for b in range(nb)]

    def load_x():
        if first:
            return jnp.where(t == nt, xs_ref[...], xp_ref[...]).reshape(rows, dm)
        return x_ref[...].reshape(rows, dm)

    @pl.when(t == 0)
    def _():
        shg_ref[...] = jnp.zeros(shg_ref.shape, _F32)
        sret_ref[...] = jnp.zeros(sret_ref.shape, _F32)

    @pl.when(t == nt)
    def _():
        shg_ref[0] = shg_in_ref[...]
        sret_ref[0] = sret_in_ref[...]

    x = load_x()
    rinv = lax.rsqrt(jnp.mean(x * x, axis=-1, keepdims=True) + NORM_EPS)
    h_s[...] = (x * nw_ref[...]).astype(_BF16)

    def proj(group, width=D_BRANCH, offset=0):
        c0 = group * D_BRANCH + offset
        return _dot(h_s[...], win_ref[:, c0:c0 + width]) * rinv

    logits = lbl_ref[...]
    ex = jnp.exp(logits - jnp.max(logits, axis=0, keepdims=True))
    prob = ex / jnp.sum(ex, axis=0, keepdims=True)
    lrow = lax.broadcasted_iota(jnp.int32, prob.shape, 0)
    lb = jnp.sum(jnp.where((lrow >= 1) & (lrow <= layer), prob, 0.0), axis=0, keepdims=True)

    q_s[...] = _silu(proj(0))
    f_gate = lb + (1.0 - lb) * _sigmoid(proj(1))
    k_s[...] = 1.0 - f_gate
    log_f = jnp.log(jnp.maximum(f_gate, F_FLOOR))
    lf_hi = log_f.astype(_BF16)
    r1 = log_f - lf_hi.astype(_F32)
    lf_mid = r1.astype(_BF16)
    lf_lo = (r1 - lf_mid.astype(_F32)).astype(_BF16)
    tril3 = tril_ref[...]
    for rs in seq_rows:
        bc_s[rs, :] = _dot(tril3, jnp.concatenate([lf_hi[rs], lf_mid[rs], lf_lo[rs]], axis=0))
    v_s[...] = proj(2).astype(_BF16)

    jrow = lax.broadcasted_iota(jnp.int32, (rows, D_HEAD), 0) & (CHUNK - 1)
    hgw = hgw_ref[...]

    ball = bc_s[...].reshape(nb, CHUNK, D_BRANCH)
    spread = jnp.max(jnp.abs(ball - ball[:, MID - 1:MID, :]))
    mid_ok = spread <= MID_SPREAD_MAX

    @pl.when(t == 0)
    def _():
        am_s[...] = jnp.zeros(am_s.shape, _F32)

    @pl.when(jnp.logical_not(mid_ok))
    def _():
        def seq_body(b, carry):
            rs = pl.ds(pl.multiple_of(b * CHUNK, CHUNK), CHUNK)
            for hd in range(N_HEADS):
                sl = slice(hd * D_HEAD, (hd + 1) * D_HEAD)
                am_s[hd, rs, :] = _factored_scores(q_s[rs, sl], k_s[rs, sl], bc_s[rs, sl], cat1_s.at[hd],
                                                   cat2_s.at[hd], p_s.at[hd], e8_ref)
            return carry

        lax.fori_loop(0, nb, seq_body, 0)

    causal = (lax.broadcasted_iota(jnp.int32, (CHUNK, CHUNK), 0)
              >= lax.broadcasted_iota(jnp.int32, (CHUNK, CHUNK), 1))

    def dense_rq():
        rq_s[...] = proj(4)

    def dense_rk():
        rk_s[...] = proj(5)

    def dense_rv():
        rv_s[...] = proj(6).astype(_BF16)

    def dense_hgate():
        sga_s[...] = _silu(proj(3))

    hg_dense = (dense_rq, dense_rk, dense_rv, dense_hgate)

    for hd in range(N_HEADS):
        hg_dense[hd]()
        sl = slice(hd * D_HEAD, (hd + 1) * D_HEAD)
        qh = q_s[:, sl]
        kh = k_s[:, sl]
        bh = bc_s[:, sl]
        vh = v_s[:, sl]
        b3 = bh.reshape(nb, CHUNK, D_HEAD)
        q3 = qh.reshape(b3.shape)
        k3 = kh.reshape(b3.shape)
        blast = b3[:, CHUNK - 1:CHUNK, :]
        bmid = b3[:, MID - 1:MID, :]
        qe = (qh * jnp.exp(bh)).astype(_BF16)
        kd = (k3 * jnp.exp(blast - b3)).reshape(rows, D_HEAD).astype(_BF16)
        qm = (q3 * jnp.exp(b3 - bmid)).reshape(rows, D_HEAD).astype(_BF16)
        km = (k3 * jnp.exp(bmid - b3)).reshape(rows, D_HEAD).astype(_BF16)
        chunk_dec = jnp.exp(blast)
        states = [shg_ref[0, b, hd] for b in range(nb)]
        amats = [jnp.where(mid_ok, jnp.where(causal, _dot_nt(qm[rs], km[rs]), 0.0), am_s[hd, rs, :])
                 for rs in seq_rows]
        outs = [_dot(amats[b].astype(_BF16), vh[rs]) + _dot_nt(qe[rs], states[b].astype(_BF16))
                for b, rs in enumerate(seq_rows)]
        for b, rs in enumerate(seq_rows):
            shg_ref[0, b, hd] = states[b] * chunk_dec[b] + _dot_tn(vh[rs], kd[rs])
        oa_s[:, sl] = _rms_rows(jnp.concatenate(outs, axis=0), hgw)

    pos0 = jnp.where(t == nt, PAST_LEN, t * CHUNK)
    posf = (lax.broadcasted_iota(jnp.int32, (CHUNK, D_HEAD), 0) + pos0).astype(_F32)
    ang = posf * inv_ref[...]
    lane = lax.broadcasted_iota(jnp.int32, (CHUNK, D_HEAD), 1)
    cosf = jnp.cos(ang)
    sinf = jnp.where(lane < D_HEAD // 2, -1.0, 1.0) * jnp.sin(ang)
    cos3 = jnp.broadcast_to(cosf[None], (nb, CHUNK, D_HEAD)).reshape(rows, D_HEAD)
    sin3 = jnp.broadcast_to(sinf[None], (nb, CHUNK, D_HEAD)).reshape(rows, D_HEAD)

    def rope(v):
        return v * cos3 + pltpu.roll(v, D_HEAD // 2, 1) * sin3

    def dense_rgate_za():
        sgr_s[...] = _silu(proj(7))
        za_s[...] = _dot((oa_s[...] * sga_s[...]).astype(_BF16), wbr_ref[0])

    def dense_gate_a():
        m_s[...] = _sigmoid(proj(8, dm)) * za_s[...]

    def dense_gate_b0():
        gb_s[:, 0:dm // 2] = _sigmoid(proj(8, dm // 2, dm))

    def dense_gate_b1():
        gb_s[:, dm // 2:dm] = _sigmoid(proj(8, dm // 2, dm + dm // 2))

    ret_dense = (dense_rgate_za, dense_gate_a, dense_gate_b0, dense_gate_b1)

    idx_r = jrow.astype(_F32)
    rel = (lax.broadcasted_iota(jnp.int32, (CHUNK, CHUNK), 0)
           - lax.broadcasted_iota(jnp.int32, (CHUNK, CHUNK), 1)).astype(_F32)
    rnw = rnw_ref[...]
    rnb = rnb_ref[...]
    for hd in range(N_HEADS):
        ret_dense[hd]()
        sl = slice(hd * D_HEAD, (hd + 1) * D_HEAD)
        lg = math.log(1.0 - 2.0 ** (-5.0 - hd))
        kf = rope(rk_s[:, sl]) * (D_HEAD ** -0.5)
        qf = rope(rq_s[:, sl])
        qb = qf.astype(_BF16)
        kb = kf.astype(_BF16)
        kdec = (kf * jnp.exp(lg * (CHUNK - 1.0 - idx_r))).astype(_BF16)
        qdec = (qf * jnp.exp(lg * (idx_r + 1.0))).astype(_BF16)
        vh = rv_s[:, sl]
        dmat = jnp.where(rel >= 0, jnp.exp(lg * jnp.maximum(rel, 0.0)), 0.0)
        states = [sret_ref[0, b, hd] for b in range(nb)]
        amats = [_dot_nt(qb[rs], kb[rs]) * dmat for rs in seq_rows]
        outs = [_dot(jnp.concatenate([qdec[rs], amats[b].astype(_BF16)], axis=1),
                     jnp.concatenate([states[b].astype(_BF16), vh[rs]], axis=0))
                for b, rs in enumerate(seq_rows)]
        for b, rs in enumerate(seq_rows):
            sret_ref[0, b, hd] = math.exp(lg * CHUNK) * states[b] + _dot_tn(kdec[rs], vh[rs])
        o = jnp.concatenate(outs, axis=0)
        mu = jnp.mean(o, axis=-1, keepdims=True)
        oc = o - mu
        var = jnp.mean(oc * oc, axis=-1, keepdims=True)
        ob_s[:, sl] = oc * lax.rsqrt(var + GN_EPS) * rnw[:, sl] + rnb[:, sl]

    z_b = _dot((ob_s[...] * sgr_s[...]).astype(_BF16), wbr_ref[1])
    merged = m_s[...] + gb_s[...] * z_b
    y = load_x() + _dot(merged.astype(_BF16), wout_ref[...])
    y_ref[...] = y.reshape(y_ref.shape)


def _layer_spec(block):
    zeros = (0,) * (len(block) - 1)
    return pl.BlockSpec(block, lambda t, l: (l[0],) + zeros, pipeline_mode=pl.Buffered(1))


def _const_spec(block):
    zeros = (0,) * len(block)
    return pl.BlockSpec(block, lambda t, l: zeros, pipeline_mode=pl.Buffered(1))


def _mixer_call(nt, lv, xs_in, shg_in, sret_in, norm_w, w_in, lb_logits, hg_norm_w, ret_norm_w,
                ret_norm_b, w_branch, w_out, inv2, tril3, e8):
    first = len(xs_in) == 2
    nb, _, dm = xs_in[0].shape
    depth = w_in.shape[0]
    rows = nb * CHUNK
    d_in = w_in.shape[2]
    st_block = (None, nb, N_HEADS, D_HEAD, D_HEAD)
    x_spec = pl.BlockSpec((nb, CHUNK, dm), lambda t, l: (0, t, 0))
    if first:
        x_specs = [pl.BlockSpec((nb, CHUNK, dm), lambda t, l: (0, jnp.minimum(t, nt - 1), 0)),
                   _const_spec((nb, CHUNK, dm))]
    else:
        x_specs = [x_spec]
    in_specs = x_specs + [
        _layer_spec(st_block),
        _layer_spec(st_block),
        _layer_spec((None, 1, dm)),
        _layer_spec((None, dm, d_in)),
        _const_spec((depth, D_BRANCH)),
        _layer_spec((None, 1, D_HEAD)),
        _layer_spec((None, 1, D_BRANCH)),
        _layer_spec((None, 1, D_BRANCH)),
        _layer_spec((None, 2, D_BRANCH, dm)),
        _layer_spec((None, dm, dm)),
        _const_spec((1, D_HEAD)),
        _const_spec((CHUNK, 3 * CHUNK)),
        _const_spec((SUB * D_HEAD, D_HEAD)),
    ]
    st_out = jax.ShapeDtypeStruct((2, nb, N_HEADS, D_HEAD, D_HEAD), _F32)
    st_out_spec = pl.BlockSpec((1, nb, N_HEADS, D_HEAD, D_HEAD), lambda t, l: (t // nt, 0, 0, 0, 0))
    scratch = [
        pltpu.VMEM((rows, dm), _BF16),
        pltpu.VMEM((rows, D_BRANCH), _F32),
        pltpu.VMEM((rows, D_BRANCH), _F32),
        pltpu.VMEM((rows, D_BRANCH), _F32),
        pltpu.VMEM((rows, D_BRANCH), _BF16),
        pltpu.VMEM((rows, D_BRANCH), _F32),
        pltpu.VMEM((rows, D_BRANCH), _F32),
        pltpu.VMEM((rows, D_BRANCH), _BF16),
        pltpu.VMEM((rows, D_BRANCH), _F32),
        pltpu.VMEM((rows, D_BRANCH), _F32),
        pltpu.VMEM((rows, D_BRANCH), _F32),
        pltpu.VMEM((rows, D_BRANCH), _F32),
        pltpu.VMEM((rows, dm), _F32),
        pltpu.VMEM((rows, dm), _F32),
        pltpu.VMEM((rows, dm), _F32),
        pltpu.VMEM((N_HEADS, CHUNK, N_SLOT * D_HEAD), _BF16),
        pltpu.VMEM((N_HEADS, CHUNK, N_SLOT * D_HEAD), _BF16),
        pltpu.VMEM((N_HEADS, CHUNK, SUB * D_HEAD), _BF16),
        pltpu.VMEM((N_HEADS, rows, CHUNK), _F32),
    ]
    grid_spec = pltpu.PrefetchScalarGridSpec(
        num_scalar_prefetch=1, grid=(nt + 1,), in_specs=in_specs,
        out_specs=[x_spec, st_out_spec, st_out_spec], scratch_shapes=scratch)
    return pl.pallas_call(
        functools.partial(_mixer_kernel, nt, first),
        grid_spec=grid_spec,
        out_shape=[jax.ShapeDtypeStruct((nb, (nt + 1) * CHUNK, dm), _F32), st_out, st_out],
        input_output_aliases={} if first else {1: 0},
        compiler_params=pltpu.CompilerParams(dimension_semantics=("arbitrary",),
                                             vmem_limit_bytes=VMEM_LIMIT_BYTES),
        name="mixer_first" if first else "mixer",
    )(lv, *xs_in, shg_in, sret_in, norm_w, w_in, lb_logits, hg_norm_w, ret_norm_w, ret_norm_b, w_branch,
      w_out, inv2, tril3, e8)


def _ffn_kernel(nt, last, l_ref, x_ref, cc_in_ref, nw_ref, wup_ref, cw_ref, cb_ref, wd_ref, nfw_ref, *refs):
    if last:
        yp_ref, ys_ref, cc_ref, h_s, acc_s, ext_s, g_s = refs
    else:
        y_ref, cc_ref, h_s, acc_s, ext_s, g_s = refs
    t = pl.program_id(0)
    nb, _, dm = x_ref.shape
    rows = nb * CHUNK
    d_ff = wd_ref.shape[0]
    fb = FFN_BLOCK
    nblk = d_ff // fb

    @pl.when(t == 0)
    def _():
        cc_ref[...] = jnp.zeros(cc_ref.shape, _F32)

    @pl.when(t == nt)
    def _():
        cc_ref[0] = cc_in_ref[...]

    x = x_ref[...].reshape(rows, dm)
    rinv = lax.rsqrt(jnp.mean(x * x, axis=-1, keepdims=True) + NORM_EPS)
    h_s[...] = (x * nw_ref[...]).astype(_BF16)

    def up(i):
        h = h_s[...]
        cols = slice(i * fb, (i + 1) * fb)
        gcols = slice(d_ff + i * fb, d_ff + (i + 1) * fb)
        ext_s[i % 2, :, 8:8 + CHUNK, :] = (_dot(h, wup_ref[:, cols]) * rinv).reshape(nb, CHUNK, fb)
        g_s[i % 2] = _dot(h, wup_ref[:, gcols]) * rinv

    def down(i):
        cols = slice(i * fb, (i + 1) * fb)
        ext = ext_s.at[i % 2]
        ext[:, 6:8, :] = cc_ref[0, :, :, cols]
        a3 = ext[:, 8:8 + CHUNK, :]
        cw = cw_ref[:, cols]
        conv = (cb_ref[:, cols] + ext[:, 6:6 + CHUNK, :] * cw[0:1, :] + ext[:, 7:7 + CHUNK, :] * cw[1:2, :]
                + a3 * cw[2:3, :])
        cc_ref[0, :, :, cols] = a3[:, CHUNK - 2:CHUNK, :]
        u = (_silu(conv).reshape(rows, fb) * g_s[i % 2]).astype(_BF16)
        return _dot(u, wd_ref[cols, :])

    up(0)
    y = None
    for i in range(nblk):
        if i + 1 < nblk:
            up(i + 1)
        part = down(i)
        prev = x_ref[...].reshape(rows, dm) if i == 0 else acc_s[...]
        if i + 1 < nblk:
            acc_s[...] = prev + part
        else:
            y = prev + part

    if last:
        yn = _rms_rows(y, nfw_ref[...]).reshape(nb, CHUNK, dm)

        @pl.when(t < nt)
        def _():
            yp_ref[...] = yn

        @pl.when(t == nt)
        def _():
            ys_ref[...] = yn
    else:
        y_ref[...] = y.reshape(y_ref.shape)


def _ffn_call(nt, last, lv, x, cc_in, norm_w, w_up, cw, cb, wd, nfw):
    nb, _, dm = x.shape
    rows = nb * CHUNK
    d_ff = wd.shape[1]
    x_spec = pl.BlockSpec((nb, CHUNK, dm), lambda t, l: (0, t, 0))
    in_specs = [
        x_spec,
        _layer_spec((None, nb, 2, d_ff)),
        _layer_spec((None, 1, dm)),
        _layer_spec((None, dm, 2 * d_ff)),
        _layer_spec((None, 3, d_ff)),
        _layer_spec((None, 1, d_ff)),
        _layer_spec((None, d_ff, dm)),
        _const_spec((1, dm)),
    ]
    cc_out = jax.ShapeDtypeStruct((2, nb, 2, d_ff), _F32)
    cc_spec = pl.BlockSpec((1, nb, 2, d_ff), lambda t, l: (t // nt, 0, 0, 0))
    if last:
        out_shape = [jax.ShapeDtypeStruct((nb, nt * CHUNK, dm), _F32), jax.ShapeDtypeStruct((nb, CHUNK, dm), _F32), cc_out]
        out_specs = [pl.BlockSpec((nb, CHUNK, dm), lambda t, l: (0, jnp.minimum(t, nt - 1), 0)),
                     pl.BlockSpec((nb, CHUNK, dm), lambda t, l: (0, 0, 0)), cc_spec]
    else:
        out_shape = [jax.ShapeDtypeStruct(x.shape, _F32), cc_out]
        out_specs = [x_spec, cc_spec]
    scratch = [
        pltpu.VMEM((rows, dm), _BF16),
        pltpu.VMEM((rows, dm), _F32),
        pltpu.VMEM((2, nb, CHUNK + 8, FFN_BLOCK), _F32),
        pltpu.VMEM((2, rows, FFN_BLOCK), _F32),
    ]
    grid_spec = pltpu.PrefetchScalarGridSpec(
        num_scalar_prefetch=1, grid=(nt + 1,), in_specs=in_specs, out_specs=out_specs, scratch_shapes=scratch)
    return pl.pallas_call(
        functools.partial(_ffn_kernel, nt, last),
        grid_spec=grid_spec,
        out_shape=out_shape,
        input_output_aliases={} if last else {1: 0},
        compiler_params=pltpu.CompilerParams(dimension_semantics=("arbitrary",),
                                             vmem_limit_bytes=VMEM_LIMIT_BYTES),
        name="ffn_last" if last else "ffn",
    )(lv, x, cc_in, norm_w, w_up, cw, cb, wd, nfw)


def _constants():
    half = D_HEAD // 2
    inv = ROPE_BASE ** (-jnp.arange(half, dtype=_F32) / half)
    inv2 = jnp.concatenate([inv, inv])[None, :]
    r = np.arange(CHUNK)
    tril = (r[:, None] >= r[None, :]).astype(np.float32)
    tril3 = jnp.asarray(np.concatenate([tril, tril, tril], axis=1), _BF16)
    e8 = np.zeros((SUB * D_HEAD, D_HEAD), np.float32)
    cols = np.arange(CHUNK)
    for s in range(SUB):
        e8[s * D_HEAD:(s + 1) * D_HEAD, cols[cols % SUB == s]] = 1.0
    return inv2, tril3, jnp.asarray(e8, _BF16)


def kernel(x_prompt, x_sample, state_hgrn, state_ret, cache_conv, norm_mix_w, w_in, hg_lb_logits,
           hg_norm_w, ret_norm_w, ret_norm_b, w_branch, w_out, norm_ffn_w, w_up, conv_w, conv_b,
           w_down, norm_final_w):
    nb, tp, dm = x_prompt.shape
    depth = w_in.shape[0]
    d_ff = w_down.shape[1]
    assert x_sample.shape == (nb, CHUNK, dm) and tp % CHUNK == 0 and d_ff % FFN_BLOCK == 0
    nt = tp // CHUNK
    inv2, tril3, e8 = _constants()

    w_in_b = w_in.astype(_BF16)
    w_branch_b = w_branch.astype(_BF16)
    w_out_b = w_out.astype(_BF16)
    w_up_b = w_up.astype(_BF16)
    w_down_b = w_down.astype(_BF16)
    shg_in = jnp.swapaxes(state_hgrn, -1, -2)
    nmw = norm_mix_w[:, None, :]
    nfw_l = norm_ffn_w[:, None, :]
    hgw = hg_norm_w[:, None, :]
    rnw = ret_norm_w[:, None, :]
    rnb = ret_norm_b[:, None, :]
    cb = conv_b[:, None, :]
    nfin = norm_final_w[None, :]

    xs = (x_prompt, x_sample)
    hg_all, ret_all, cc_all = [], [], []
    for l in range(depth):
        lv = jnp.full((1,), l, jnp.int32)
        x, s_hg, s_ret = _mixer_call(nt, lv, xs, shg_in, state_ret, nmw, w_in_b, hg_lb_logits, hgw, rnw, rnb,
                                     w_branch_b, w_out_b, inv2, tril3, e8)
        *xs, s_cc = _ffn_call(nt, l == depth - 1, lv, x, cache_conv, nfw_l, w_up_b, conv_w, cb, w_down_b, nfin)
        hg_all.append(s_hg)
        ret_all.append(s_ret)
        cc_all.append(s_cc)
    y_prompt, y_sample = xs
    hg_all = jnp.swapaxes(jnp.stack(hg_all), -1, -2)
    ret_all = jnp.stack(ret_all)
    cc_all = jnp.stack(cc_all)
    return (y_prompt, y_sample, hg_all[:, 0], ret_all[:, 0], cc_all[:, 0],
            hg_all[:, 1], ret_all[:, 1], cc_all[:, 1])
```

```python
import functools
import math

import numpy as np
import jax
import jax.numpy as jnp
from jax import lax
from jax.experimental import pallas as pl
from jax.experimental.pallas import tpu as pltpu

CHUNK = 64
SUB = 8
LEVELS = (64, 32, 16)
N_SLOT = sum(CHUNK // m for m in LEVELS)
N_HEADS = 4
D_HEAD = 128
D_BRANCH = N_HEADS * D_HEAD
PAST_LEN = 2048
ROPE_BASE = 10000.0
NORM_EPS = 1e-6
GN_EPS = 1e-5
F_FLOOR = 1e-30
MID = CHUNK // 2
MID_SPREAD_MAX = 80.0
FFN_BLOCK = 256
VMEM_LIMIT_BYTES = 60 * 1024 * 1024

_F32 = jnp.float32
_BF16 = jnp.bfloat16


def _dot(a, b):
    return jnp.dot(a, b, preferred_element_type=_F32)


def _dot_nt(a, b):
    return lax.dot_general(a, b, (((1,), (1,)), ((), ())), preferred_element_type=_F32)


def _dot_tn(a, b):
    return lax.dot_general(a, b, (((0,), (0,)), ((), ())), preferred_element_type=_F32)


def _sigmoid(x):
    return jax.nn.sigmoid(x)


def _silu(x):
    return x * jax.nn.sigmoid(x)


def _rms_rows(x, w):
    ms = jnp.mean(x * x, axis=-1, keepdims=True)
    return x * lax.rsqrt(ms + NORM_EPS) * w


def _factored_scores(qh, kh, bh, cat1, cat2, p_buf, e8_ref):
    rows = qh.shape[0]
    jrow = lax.broadcasted_iota(jnp.int32, (rows, D_HEAD), 0)
    slot = 0
    for m in LEVELS:
        half = m // 2
        bm = bh.reshape(rows // m, m, D_HEAD)
        delta = bm - bm[:, half - 1:half, :]
        is_q = lax.broadcasted_iota(jnp.int32, bm.shape, 1) >= half
        dec = jnp.exp(jnp.minimum(jnp.where(is_q, delta, -delta), 0.0))
        xm = (jnp.where(is_q, qh.reshape(bm.shape), kh.reshape(bm.shape)) * dec).reshape(rows, D_HEAD)
        hidx = jrow >> int(math.log2(half))
        for s in range(rows // m):
            cs = slice(slot * D_HEAD, (slot + 1) * D_HEAD)
            cat1[:, cs] = jnp.where(hidx == 2 * s + 1, xm, 0.0).astype(_BF16)
            cat2[:, cs] = jnp.where(hidx == 2 * s, xm, 0.0).astype(_BF16)
            slot += 1
    rowg = lax.broadcasted_iota(jnp.int32, (rows // SUB, SUB, D_HEAD), 1)
    qg = qh.reshape(rowg.shape)
    kg = kh.reshape(rowg.shape)
    bg = bh.reshape(rowg.shape)
    for s in range(SUB):
        msk = rowg >= s
        dec = jnp.exp(jnp.where(msk, bg - bg[:, s:s + 1, :], 0.0))
        ps = jnp.where(msk, dec * qg * kg[:, s:s + 1, :], 0.0)
        p_buf[:, s * D_HEAD:(s + 1) * D_HEAD] = ps.reshape(rows, D_HEAD).astype(_BF16)
    ad_col = lax.broadcasted_iota(jnp.int32, (rows, D_HEAD), 1)
    ad = jnp.where((ad_col >> 3) == (jrow >> 3), _dot(p_buf[...], e8_ref[...]), 0.0)
    return _dot_nt(cat1[...], cat2[...]) + ad[:, 0:CHUNK]


def _mixer_kernel(nt, first, l_ref, *refs):
    if first:
        xp_ref, xs_ref, *refs = refs
    else:
        x_ref, *refs = refs
    (shg_in_ref, sret_in_ref, nw_ref, win_ref, lbl_ref, hgw_ref, rnw_ref, rnb_ref, wbr_ref, wout_ref, inv_ref,
     tril_ref, e8_ref,
     y_ref, shg_ref, sret_ref,
     h_s, q_s, k_s, bc_s, v_s, rq_s, rk_s, rv_s, sga_s, sgr_s, oa_s, ob_s, za_s, m_s, gb_s,
     cat1_s, cat2_s, p_s, am_s) = refs
    t = pl.program_id(0)
    layer = l_ref[0]
    nb, _, dm = y_ref.shape
    rows = nb * CHUNK
    seq_rows = [slice(b * CHUNK, (b + 1) * CHUNK) for b in range(nb)]

    def load_x():
        if first:
            return jnp.where(t == nt, xs_ref[...], xp_ref[...]).reshape(rows, dm)
        return x_ref[...].reshape(rows, dm)

    @pl.when(t == 0)
    def _():
        shg_ref[...] = jnp.zeros(shg_ref.shape, _F32)
        sret_ref[...] = jnp.zeros(sret_ref.shape, _F32)

    @pl.when(t == nt)
    def _():
        shg_ref[0] = shg_in_ref[...]
        sret_ref[0] = sret_in_ref[...]

    h_s[...] = _rms_rows(load_x(), nw_ref[...]).astype(_BF16)

    def proj(group, width=D_BRANCH, offset=0):
        c0 = group * D_BRANCH + offset
        return _dot(h_s[...], win_ref[:, c0:c0 + width])

    logits = lbl_ref[...]
    ex = jnp.exp(logits - jnp.max(logits, axis=0, keepdims=True))
    prob = ex / jnp.sum(ex, axis=0, keepdims=True)
    lrow = lax.broadcasted_iota(jnp.int32, prob.shape, 0)
    lb = jnp.sum(jnp.where((lrow >= 1) & (lrow <= layer), prob, 0.0), axis=0, keepdims=True)

    f_gate = lb + (1.0 - lb) * _sigmoid(proj(1))
    q_s[...] = _silu(proj(0))
    k_s[...] = 1.0 - f_gate
    log_f = jnp.log(jnp.maximum(f_gate, F_FLOOR))
    lf_hi = log_f.astype(_BF16)
    r1 = log_f - lf_hi.astype(_F32)
    lf_mid = r1.astype(_BF16)
    lf_lo = (r1 - lf_mid.astype(_F32)).astype(_BF16)
    tril3 = tril_ref[...]
    for rs in seq_rows:
        bc_s[rs, :] = _dot(tril3, jnp.concatenate([lf_hi[rs], lf_mid[rs], lf_lo[rs]], axis=0))
    v_s[...] = proj(2).astype(_BF16)

    jrow = lax.broadcasted_iota(jnp.int32, (rows, D_HEAD), 0) & (CHUNK - 1)
    hgw = hgw_ref[...]

    ball = bc_s[...].reshape(nb, CHUNK, D_BRANCH)
    spread = jnp.max(jnp.abs(ball - ball[:, MID - 1:MID, :]))
    mid_ok = spread <= MID_SPREAD_MAX

    @pl.when(t == 0)
    def _():
        am_s[...] = jnp.zeros(am_s.shape, _F32)

    @pl.when(jnp.logical_not(mid_ok))
    def _():
        def seq_body(b, carry):
            rs = pl.ds(pl.multiple_of(b * CHUNK, CHUNK), CHUNK)
            for hd in range(N_HEADS):
                sl = slice(hd * D_HEAD, (hd + 1) * D_HEAD)
                am_s[hd, rs, :] = _factored_scores(q_s[rs, sl], k_s[rs, sl], bc_s[rs, sl], cat1_s.at[hd],
                                                   cat2_s.at[hd], p_s.at[hd], e8_ref)
            return carry

        lax.fori_loop(0, nb, seq_body, 0)

    causal = (lax.broadcasted_iota(jnp.int32, (CHUNK, CHUNK), 0)
              >= lax.broadcasted_iota(jnp.int32, (CHUNK, CHUNK), 1))

    def dense_rq():
        rq_s[...] = proj(4)

    def dense_rk():
        rk_s[...] = proj(5)

    def dense_rv():
        rv_s[...] = proj(6).astype(_BF16)

    def dense_hgate():
        sga_s[...] = _silu(proj(3))

    hg_dense = (dense_rq, dense_rk, dense_rv, dense_hgate)

    def hg_factors(hd):
        sl = slice(hd * D_HEAD, (hd + 1) * D_HEAD)
        qh = q_s[:, sl]
        bh = bc_s[:, sl]
        b3 = bh.reshape(nb, CHUNK, D_HEAD)
        q3 = qh.reshape(b3.shape)
        k3 = k_s[:, sl].reshape(b3.shape)
        blast = b3[:, CHUNK - 1:CHUNK, :]
        bmid = b3[:, MID - 1:MID, :]
        qe = (qh * jnp.exp(bh)).astype(_BF16)
        kd = (k3 * jnp.exp(blast - b3)).reshape(rows, D_HEAD).astype(_BF16)
        qm = (q3 * jnp.exp(b3 - bmid)).reshape(rows, D_HEAD).astype(_BF16)
        km = (k3 * jnp.exp(bmid - b3)).reshape(rows, D_HEAD).astype(_BF16)
        return qe, kd, qm, km, jnp.exp(blast)

    def hg_state_matmuls(hd, factors):
        qe, kd, qm, km, chunk_dec = factors
        sl = slice(hd * D_HEAD, (hd + 1) * D_HEAD)
        vh = v_s[:, sl]
        states = [shg_ref[0, b, hd] for b in range(nb)]
        amats = [jnp.where(mid_ok, jnp.where(causal, _dot_nt(qm[rs], km[rs]), 0.0), am_s[hd, rs, :])
                 for rs in seq_rows]
        outs = [_dot(amats[b].astype(_BF16), vh[rs]) + _dot_nt(qe[rs], states[b].astype(_BF16))
                for b, rs in enumerate(seq_rows)]
        for b, rs in enumerate(seq_rows):
            shg_ref[0, b, hd] = states[b] * chunk_dec[b] + _dot_tn(vh[rs], kd[rs])
        oa_s[:, sl] = _rms_rows(jnp.concatenate(outs, axis=0), hgw)

    factors = hg_factors(0)
    for hd in range(N_HEADS):
        hg_dense[hd]()
        upcoming = hg_factors(hd + 1) if hd + 1 < N_HEADS else None
        hg_state_matmuls(hd, factors)
        factors = upcoming

    pos0 = jnp.where(t == nt, PAST_LEN, t * CHUNK)
    posf = (lax.broadcasted_iota(jnp.int32, (CHUNK, D_HEAD), 0) + pos0).astype(_F32)
    ang = posf * inv_ref[...]
    lane = lax.broadcasted_iota(jnp.int32, (CHUNK, D_HEAD), 1)
    cosf = jnp.cos(ang)
    sinf = jnp.where(lane < D_HEAD // 2, -1.0, 1.0) * jnp.sin(ang)
    cos3 = jnp.broadcast_to(cosf[None], (nb, CHUNK, D_HEAD)).reshape(rows, D_HEAD)
    sin3 = jnp.broadcast_to(sinf[None], (nb, CHUNK, D_HEAD)).reshape(rows, D_HEAD)

    def rope(v):
        return v * cos3 + pltpu.roll(v, D_HEAD // 2, 1) * sin3

    def dense_rgate_za():
        sgr_s[...] = _silu(proj(7))
        za_s[...] = _dot((oa_s[...] * sga_s[...]).astype(_BF16), wbr_ref[0])

    def dense_gate_a():
        m_s[...] = _sigmoid(proj(8, dm)) * za_s[...]

    def dense_gate_b0():
        gb_s[:, 0:dm // 2] = _sigmoid(proj(8, dm // 2, dm))

    def dense_gate_b1():
        gb_s[:, dm // 2:dm] = _sigmoid(proj(8, dm // 2, dm + dm // 2))

    ret_dense = (dense_rgate_za, dense_gate_a, dense_gate_b0, dense_gate_b1)

    idx_r = jrow.astype(_F32)
    rel = (lax.broadcasted_iota(jnp.int32, (CHUNK, CHUNK), 0)
           - lax.broadcasted_iota(jnp.int32, (CHUNK, CHUNK), 1)).astype(_F32)
    rnw = rnw_ref[...]
    rnb = rnb_ref[...]
    log_gamma = [math.log(1.0 - 2.0 ** (-5.0 - hd)) for hd in range(N_HEADS)]

    def ret_factors(hd):
        sl = slice(hd * D_HEAD, (hd + 1) * D_HEAD)
        lg = log_gamma[hd]
        kf = rope(rk_s[:, sl]) * (D_HEAD ** -0.5)
        qf = rope(rq_s[:, sl])
        kdec = (kf * jnp.exp(lg * (CHUNK - 1.0 - idx_r))).astype(_BF16)
        qdec = (qf * jnp.exp(lg * (idx_r + 1.0))).astype(_BF16)
        dmat = jnp.where(rel >= 0, jnp.exp(lg * jnp.maximum(rel, 0.0)), 0.0)
        return qf.astype(_BF16), kf.astype(_BF16), kdec, qdec, dmat

    def ret_state_matmuls(hd, factors):
        qb, kb, kdec, qdec, dmat = factors
        sl = slice(hd * D_HEAD, (hd + 1) * D_HEAD)
        vh = rv_s[:, sl]
        states = [sret_ref[0, b, hd] for b in range(nb)]
        amats = [_dot_nt(qb[rs], kb[rs]) * dmat for rs in seq_rows]
        outs = [_dot(amats[b].astype(_BF16), vh[rs]) + _dot(qdec[rs], states[b].astype(_BF16))
                for b, rs in enumerate(seq_rows)]
        for b, rs in enumerate(seq_rows):
            sret_ref[0, b, hd] = math.exp(log_gamma[hd] * CHUNK) * states[b] + _dot_tn(kdec[rs], vh[rs])
        o = jnp.concatenate(outs, axis=0)
        mu = jnp.mean(o, axis=-1, keepdims=True)
        oc = o - mu
        var = jnp.mean(oc * oc, axis=-1, keepdims=True)
        ob_s[:, sl] = oc * lax.rsqrt(var + GN_EPS) * rnw[:, sl] + rnb[:, sl]

    factors = ret_factors(0)
    for hd in range(N_HEADS):
        ret_dense[hd]()
        upcoming = ret_factors(hd + 1) if hd + 1 < N_HEADS else None
        ret_state_matmuls(hd, factors)
        factors = upcoming

    z_b = _dot((ob_s[...] * sgr_s[...]).astype(_BF16), wbr_ref[1])
    merged = m_s[...] + gb_s[...] * z_b
    y = load_x() + _dot(merged.astype(_BF16), wout_ref[...])
    y_ref[...] = y.reshape(y_ref.shape)


def _layer_spec(block):
    zeros = (0,) * (len(block) - 1)
    return pl.BlockSpec(block, lambda t, l: (l[0],) + zeros, pipeline_mode=pl.Buffered(1))


def _const_spec(block):
    zeros = (0,) * len(block)
    return pl.BlockSpec(block, lambda t, l: zeros, pipeline_mode=pl.Buffered(1))


def _mixer_call(nt, lv, xs_in, shg_in, sret_in, norm_w, w_in, lb_logits, hg_norm_w, ret_norm_w,
                ret_norm_b, w_branch, w_out, inv2, tril3, e8):
    first = len(xs_in) == 2
    nb, _, dm = xs_in[0].shape
    depth = w_in.shape[0]
    rows = nb * CHUNK
    d_in = w_in.shape[2]
    st_block = (None, nb, N_HEADS, D_HEAD, D_HEAD)
    x_spec = pl.BlockSpec((nb, CHUNK, dm), lambda t, l: (0, t, 0))
    if first:
        x_specs = [pl.BlockSpec((nb, CHUNK, dm), lambda t, l: (0, jnp.minimum(t, nt - 1), 0)),
                   _const_spec((nb, CHUNK, dm))]
    else:
        x_specs = [x_spec]
    in_specs = x_specs + [
        _layer_spec(st_block),
        _layer_spec(st_block),
        _layer_spec((None, 1, dm)),
        _layer_spec((None, dm, d_in)),
        _const_spec((depth, D_BRANCH)),
        _layer_spec((None, 1, D_HEAD)),
        _layer_spec((None, 1, D_BRANCH)),
        _layer_spec((None, 1, D_BRANCH)),
        _layer_spec((None, 2, D_BRANCH, dm)),
        _layer_spec((None, dm, dm)),
        _const_spec((1, D_HEAD)),
        _const_spec((CHUNK, 3 * CHUNK)),
        _const_spec((SUB * D_HEAD, D_HEAD)),
    ]
    st_out = jax.ShapeDtypeStruct((2, nb, N_HEADS, D_HEAD, D_HEAD), _F32)
    st_out_spec = pl.BlockSpec((1, nb, N_HEADS, D_HEAD, D_HEAD), lambda t, l: (t // nt, 0, 0, 0, 0))
    scratch = [
        pltpu.VMEM((rows, dm), _BF16),
        pltpu.VMEM((rows, D_BRANCH), _F32),
        pltpu.VMEM((rows, D_BRANCH), _F32),
        pltpu.VMEM((rows, D_BRANCH), _F32),
        pltpu.VMEM((rows, D_BRANCH), _BF16),
        pltpu.VMEM((rows, D_BRANCH), _F32),
        pltpu.VMEM((rows, D_BRANCH), _F32),
        pltpu.VMEM((rows, D_BRANCH), _BF16),
        pltpu.VMEM((rows, D_BRANCH), _F32),
        pltpu.VMEM((rows, D_BRANCH), _F32),
        pltpu.VMEM((rows, D_BRANCH), _F32),
        pltpu.VMEM((rows, D_BRANCH), _F32),
        pltpu.VMEM((rows, dm), _F32),
        pltpu.VMEM((rows, dm), _F32),
        pltpu.VMEM((rows, dm), _F32),
        pltpu.VMEM((N_HEADS, CHUNK, N_SLOT * D_HEAD), _BF16),
        pltpu.VMEM((N_HEADS, CHUNK, N_SLOT * D_HEAD), _BF16),
        pltpu.VMEM((N_HEADS, CHUNK, SUB * D_HEAD), _BF16),
        pltpu.VMEM((N_HEADS, rows, CHUNK), _F32),
    ]
    grid_spec = pltpu.PrefetchScalarGridSpec(
        num_scalar_prefetch=1, grid=(nt + 1,), in_specs=in_specs,
        out_specs=[x_spec, st_out_spec, st_out_spec], scratch_shapes=scratch)
    return pl.pallas_call(
        functools.partial(_mixer_kernel, nt, first),
        grid_spec=grid_spec,
        out_shape=[jax.ShapeDtypeStruct((nb, (nt + 1) * CHUNK, dm), _F32), st_out, st_out],
        input_output_aliases={} if first else {1: 0},
        compiler_params=pltpu.CompilerParams(dimension_semantics=("arbitrary",),
                                             vmem_limit_bytes=VMEM_LIMIT_BYTES),
        name="mixer_first" if first else "mixer",
    )(lv, *xs_in, shg_in, sret_in, norm_w, w_in, lb_logits, hg_norm_w, ret_norm_w, ret_norm_b, w_branch,
      w_out, inv2, tril3, e8)


def _ffn_kernel(nt, last, l_ref, x_ref, cc_in_ref, nw_ref, wup_ref, cw_ref, cb_ref, wd_ref, nfw_ref, *refs):
    if last:
        yp_ref, ys_ref, cc_ref, h_s, acc_s, ext_s, g_s = refs
    else:
        y_ref, cc_ref, h_s, acc_s, ext_s, g_s = refs
    t = pl.program_id(0)
    nb, _, dm = x_ref.shape
    rows = nb * CHUNK
    d_ff = wd_ref.shape[0]
    fb = FFN_BLOCK
    nblk = d_ff // fb

    @pl.when(t == 0)
    def _():
        cc_ref[...] = jnp.zeros(cc_ref.shape, _F32)

    @pl.when(t == nt)
    def _():
        cc_ref[0] = cc_in_ref[...]

    h_s[...] = _rms_rows(x_ref[...].reshape(rows, dm), nw_ref[...]).astype(_BF16)

    def up(i):
        h = h_s[...]
        cols = slice(i * fb, (i + 1) * fb)
        gcols = slice(d_ff + i * fb, d_ff + (i + 1) * fb)
        ext_s[i % 2, :, 8:8 + CHUNK, :] = _dot(h, wup_ref[:, cols]).reshape(nb, CHUNK, fb)
        g_s[i % 2] = _dot(h, wup_ref[:, gcols])

    def down(i):
        cols = slice(i * fb, (i + 1) * fb)
        ext = ext_s.at[i % 2]
        ext[:, 6:8, :] = cc_ref[0, :, :, cols]
        a3 = ext[:, 8:8 + CHUNK, :]
        cw = cw_ref[:, cols]
        conv = (cb_ref[:, cols] + ext[:, 6:6 + CHUNK, :] * cw[0:1, :] + ext[:, 7:7 + CHUNK, :] * cw[1:2, :]
                + a3 * cw[2:3, :])
        cc_ref[0, :, :, cols] = a3[:, CHUNK - 2:CHUNK, :]
        u = (_silu(conv).reshape(rows, fb) * g_s[i % 2]).astype(_BF16)
        return _dot(u, wd_ref[cols, :])

    up(0)
    y = None
    for i in range(nblk):
        if i + 1 < nblk:
            up(i + 1)
        part = down(i)
        prev = x_ref[...].reshape(rows, dm) if i == 0 else acc_s[...]
        if i + 1 < nblk:
            acc_s[...] = prev + part
        else:
            y = prev + part

    if last:
        yn = _rms_rows(y, nfw_ref[...]).reshape(nb, CHUNK, dm)

        @pl.when(t < nt)
        def _():
            yp_ref[...] = yn

        @pl.when(t == nt)
        def _():
            ys_ref[...] = yn
    else:
        y_ref[...] = y.reshape(y_ref.shape)


def _ffn_call(nt, last, lv, x, cc_in, norm_w, w_up, cw, cb, wd, nfw):
    nb, _, dm = x.shape
    rows = nb * CHUNK
    d_ff = wd.shape[1]
    x_spec = pl.BlockSpec((nb, CHUNK, dm), lambda t, l: (0, t, 0))
    in_specs = [
        x_spec,
        _layer_spec((None, nb, 2, d_ff)),
        _layer_spec((None, 1, dm)),
        _layer_spec((None, dm, 2 * d_ff)),
        _layer_spec((None, 3, d_ff)),
        _layer_spec((None, 1, d_ff)),
        _layer_spec((None, d_ff, dm)),
        _const_spec((1, dm)),
    ]
    cc_out = jax.ShapeDtypeStruct((2, nb, 2, d_ff), _F32)
    cc_spec = pl.BlockSpec((1, nb, 2, d_ff), lambda t, l: (t // nt, 0, 0, 0))
    if last:
        out_shape = [jax.ShapeDtypeStruct((nb, nt * CHUNK, dm), _F32), jax.ShapeDtypeStruct((nb, CHUNK, dm), _F32), cc_out]
        out_specs = [pl.BlockSpec((nb, CHUNK, dm), lambda t, l: (0, jnp.minimum(t, nt - 1), 0)),
                     pl.BlockSpec((nb, CHUNK, dm), lambda t, l: (0, 0, 0)), cc_spec]
    else:
        out_shape = [jax.ShapeDtypeStruct(x.shape, _F32), cc_out]
        out_specs = [x_spec, cc_spec]
    scratch = [
        pltpu.VMEM((rows, dm), _BF16),
        pltpu.VMEM((rows, dm), _F32),
        pltpu.VMEM((2, nb, CHUNK + 8, FFN_BLOCK), _F32),
        pltpu.VMEM((2, rows, FFN_BLOCK), _F32),
    ]
    grid_spec = pltpu.PrefetchScalarGridSpec(
        num_scalar_prefetch=1, grid=(nt + 1,), in_specs=in_specs, out_specs=out_specs, scratch_shapes=scratch)
    return pl.pallas_call(
        functools.partial(_ffn_kernel, nt, last),
        grid_spec=grid_spec,
        out_shape=out_shape,
        input_output_aliases={} if last else {1: 0},
        compiler_params=pltpu.CompilerParams(dimension_semantics=("arbitrary",),
                                             vmem_limit_bytes=VMEM_LIMIT_BYTES),
        name="ffn_last" if last else "ffn",
    )(lv, x, cc_in, norm_w, w_up, cw, cb, wd, nfw)


def _constants():
    half = D_HEAD // 2
    inv = ROPE_BASE ** (-jnp.arange(half, dtype=_F32) / half)
    inv2 = jnp.concatenate([inv, inv])[None, :]
    r = np.arange(CHUNK)
    tril = (r[:, None] >= r[None, :]).astype(np.float32)
    tril3 = jnp.asarray(np.concatenate([tril, tril, tril], axis=1), _BF16)
    e8 = np.zeros((SUB * D_HEAD, D_HEAD), np.float32)
    cols = np.arange(CHUNK)
    for s in range(SUB):
        e8[s * D_HEAD:(s + 1) * D_HEAD, cols[cols % SUB == s]] = 1.0
    return inv2, tril3, jnp.asarray(e8, _BF16)


def kernel(x_prompt, x_sample, state_hgrn, state_ret, cache_conv, norm_mix_w, w_in, hg_lb_logits,
           hg_norm_w, ret_norm_w, ret_norm_b, w_branch, w_out, norm_ffn_w, w_up, conv_w, conv_b,
           w_down, norm_final_w):
    nb, tp, dm = x_prompt.shape
    depth = w_in.shape[0]
    d_ff = w_down.shape[1]
    assert x_sample.shape == (nb, CHUNK, dm) and tp % CHUNK == 0 and d_ff % FFN_BLOCK == 0
    nt = tp // CHUNK
    inv2, tril3, e8 = _constants()

    w_in_b = w_in.astype(_BF16)
    w_branch_b = w_branch.astype(_BF16)
    w_out_b = w_out.astype(_BF16)
    w_up_b = w_up.astype(_BF16)
    w_down_b = w_down.astype(_BF16)
    shg_in = jnp.swapaxes(state_hgrn, -1, -2)
    nmw = norm_mix_w[:, None, :]
    nfw_l = norm_ffn_w[:, None, :]
    hgw = hg_norm_w[:, None, :]
    rnw = ret_norm_w[:, None, :]
    rnb = ret_norm_b[:, None, :]
    cb = conv_b[:, None, :]
    nfin = norm_final_w[None, :]

    xs = (x_prompt, x_sample)
    hg_all, ret_all, cc_all = [], [], []
    for l in range(depth):
        lv = jnp.full((1,), l, jnp.int32)
        x, s_hg, s_ret = _mixer_call(nt, lv, xs, shg_in, state_ret, nmw, w_in_b, hg_lb_logits, hgw, rnw, rnb,
                                     w_branch_b, w_out_b, inv2, tril3, e8)
        *xs, s_cc = _ffn_call(nt, l == depth - 1, lv, x, cache_conv, nfw_l, w_up_b, conv_w, cb, w_down_b, nfin)
        hg_all.append(s_hg)
        ret_all.append(s_ret)
        cc_all.append(s_cc)
    y_prompt, y_sample = xs
    hg_all = jnp.swapaxes(jnp.stack(hg_all), -1, -2)
    ret_all = jnp.stack(ret_all)
    cc_all = jnp.stack(cc_all)
    return (y_prompt, y_sample, hg_all[:, 0], ret_all[:, 0], cc_all[:, 0],
            hg_all[:, 1], ret_all[:, 1], cc_all[:, 1])
```

```python
import functools
import math

import numpy as np
import jax
import jax.numpy as jnp
from jax import lax
from jax.experimental import pallas as pl
from jax.experimental.pallas import tpu as pltpu

CHUNK = 64
SUB = 8
LEVELS = (64, 32, 16)
N_SLOT = sum(CHUNK // m for m in LEVELS)
N_HEADS = 4
D_HEAD = 128
D_BRANCH = N_HEADS * D_HEAD
PAST_LEN = 2048
ROPE_BASE = 10000.0
NORM_EPS = 1e-6
GN_EPS = 1e-5
F_FLOOR = 1e-30
MID = CHUNK // 2
MID_SPREAD_MAX = 80.0
FFN_BLOCK = 256
PIECE = 256
VMEM_LIMIT_BYTES = 60 * 1024 * 1024

_F32 = jnp.float32
_BF16 = jnp.bfloat16


def _dot(a, b):
    return jnp.dot(a, b, preferred_element_type=_F32)


def _dot_nt(a, b):
    return lax.dot_general(a, b, (((1,), (1,)), ((), ())), preferred_element_type=_F32)


def _dot_tn(a, b):
    return lax.dot_general(a, b, (((0,), (0,)), ((), ())), preferred_element_type=_F32)


def _sigmoid(x):
    return jax.nn.sigmoid(x)


def _silu(x):
    return x * jax.nn.sigmoid(x)


def _run(thunks):
    for thunk in thunks:
        thunk()


def _rms_rows(x, w):
    ms = jnp.mean(x * x, axis=-1, keepdims=True)
    return x * lax.rsqrt(ms + NORM_EPS) * w


def _factored_scores(qh, kh, bh, cat1, cat2, p_buf, e8_ref):
    rows = qh.shape[0]
    jrow = lax.broadcasted_iota(jnp.int32, (rows, D_HEAD), 0)
    slot = 0
    for m in LEVELS:
        half = m // 2
        bm = bh.reshape(rows // m, m, D_HEAD)
        delta = bm - bm[:, half - 1:half, :]
        is_q = lax.broadcasted_iota(jnp.int32, bm.shape, 1) >= half
        dec = jnp.exp(jnp.minimum(jnp.where(is_q, delta, -delta), 0.0))
        xm = (jnp.where(is_q, qh.reshape(bm.shape), kh.reshape(bm.shape)) * dec).reshape(rows, D_HEAD)
        hidx = jrow >> int(math.log2(half))
        for s in range(rows // m):
            cs = slice(slot * D_HEAD, (slot + 1) * D_HEAD)
            cat1[:, cs] = jnp.where(hidx == 2 * s + 1, xm, 0.0).astype(_BF16)
            cat2[:, cs] = jnp.where(hidx == 2 * s, xm, 0.0).astype(_BF16)
            slot += 1
    rowg = lax.broadcasted_iota(jnp.int32, (rows // SUB, SUB, D_HEAD), 1)
    qg = qh.reshape(rowg.shape)
    kg = kh.reshape(rowg.shape)
    bg = bh.reshape(rowg.shape)
    for s in range(SUB):
        msk = rowg >= s
        dec = jnp.exp(jnp.where(msk, bg - bg[:, s:s + 1, :], 0.0))
        ps = jnp.where(msk, dec * qg * kg[:, s:s + 1, :], 0.0)
        p_buf[:, s * D_HEAD:(s + 1) * D_HEAD] = ps.reshape(rows, D_HEAD).astype(_BF16)
    ad_col = lax.broadcasted_iota(jnp.int32, (rows, D_HEAD), 1)
    ad = jnp.where((ad_col >> 3) == (jrow >> 3), _dot(p_buf[...], e8_ref[...]), 0.0)
    return _dot_nt(cat1[...], cat2[...]) + ad[:, 0:CHUNK]


def _mixer_kernel(nt, first, l_ref, *refs):
    if first:
        xp_ref, xs_ref, *refs = refs
    else:
        x_ref, *refs = refs
    (shg_in_ref, sret_in_ref, nw_ref, win_ref, lbl_ref, hgw_ref, rnw_ref, rnb_ref, wbr_ref, wout_ref, inv_ref,
     tril_ref, e8_ref,
     y_ref, shg_ref, sret_ref,
     h_s, q_s, k_s, bc_s, v_s, rq_s, rk_s, rv_s, sga_s, sgr_s, oa_s, ob_s, oag_s, za_s, m_s, gb_s,
     cat1_s, cat2_s, p_s, am_s) = refs
    t = pl.program_id(0)
    layer = l_ref[0]
    nb, _, dm = y_ref.shape
    rows = nb * CHUNK
    seq_rows = [slice(b * CHUNK, (b + 1) * CHUNK) for b in range(nb)]

    def load_x():
        if first:
            return jnp.where(t == nt, xs_ref[...], xp_ref[...]).reshape(rows, dm)
        return x_ref[...].reshape(rows, dm)

    @pl.when(t == 0)
    def _():
        shg_ref[...] = jnp.zeros(shg_ref.shape, _F32)
        sret_ref[...] = jnp.zeros(sret_ref.shape, _F32)

    @pl.when(t == nt)
    def _():
        shg_ref[0] = shg_in_ref[...]
        sret_ref[0] = sret_in_ref[...]

    h_s[...] = _rms_rows(load_x(), nw_ref[...]).astype(_BF16)

    def proj(group, width=D_BRANCH, offset=0):
        c0 = group * D_BRANCH + offset
        return _dot(h_s[...], win_ref[:, c0:c0 + width])

    logits = lbl_ref[...]
    ex = jnp.exp(logits - jnp.max(logits, axis=0, keepdims=True))
    prob = ex / jnp.sum(ex, axis=0, keepdims=True)
    lrow = lax.broadcasted_iota(jnp.int32, prob.shape, 0)
    lb = jnp.sum(jnp.where((lrow >= 1) & (lrow <= layer), prob, 0.0), axis=0, keepdims=True)

    f_gate = lb + (1.0 - lb) * _sigmoid(proj(1))
    q_s[...] = _silu(proj(0))
    k_s[...] = 1.0 - f_gate
    log_f = jnp.log(jnp.maximum(f_gate, F_FLOOR))
    lf_hi = log_f.astype(_BF16)
    r1 = log_f - lf_hi.astype(_F32)
    lf_mid = r1.astype(_BF16)
    lf_lo = (r1 - lf_mid.astype(_F32)).astype(_BF16)
    tril3 = tril_ref[...]
    for rs in seq_rows:
        bc_s[rs, :] = _dot(tril3, jnp.concatenate([lf_hi[rs], lf_mid[rs], lf_lo[rs]], axis=0))
    v_s[...] = proj(2).astype(_BF16)

    jrow = lax.broadcasted_iota(jnp.int32, (rows, D_HEAD), 0) & (CHUNK - 1)
    hgw = hgw_ref[...]

    ball = bc_s[...].reshape(nb, CHUNK, D_BRANCH)
    spread = jnp.max(jnp.abs(ball - ball[:, MID - 1:MID, :]))
    mid_ok = spread <= MID_SPREAD_MAX

    @pl.when(t == 0)
    def _():
        am_s[...] = jnp.zeros(am_s.shape, _F32)

    @pl.when(jnp.logical_not(mid_ok))
    def _():
        def seq_body(b, carry):
            rs = pl.ds(pl.multiple_of(b * CHUNK, CHUNK), CHUNK)
            for hd in range(N_HEADS):
                sl = slice(hd * D_HEAD, (hd + 1) * D_HEAD)
                am_s[hd, rs, :] = _factored_scores(q_s[rs, sl], k_s[rs, sl], bc_s[rs, sl], cat1_s.at[hd],
                                                   cat2_s.at[hd], p_s.at[hd], e8_ref)
            return carry

        lax.fori_loop(0, nb, seq_body, 0)

    causal = (lax.broadcasted_iota(jnp.int32, (CHUNK, CHUNK), 0)
              >= lax.broadcasted_iota(jnp.int32, (CHUNK, CHUNK), 1))

    def proj_piece(dst, group, post, c):
        dst[:, c:c + PIECE] = post(proj(group, PIECE, c))

    def to_bf16(v):
        return v.astype(_BF16)

    def identity(v):
        return v

    hg_units = ((rq_s, 4, identity), (rk_s, 5, identity), (rv_s, 6, to_bf16), (sga_s, 3, _silu))
    hg_pieces = [[functools.partial(proj_piece, dst, group, post, c) for c in range(0, D_BRANCH, PIECE)]
                 for dst, group, post in hg_units]

    def hg_factors(hd):
        sl = slice(hd * D_HEAD, (hd + 1) * D_HEAD)
        qh = q_s[:, sl]
        bh = bc_s[:, sl]
        b3 = bh.reshape(nb, CHUNK, D_HEAD)
        q3 = qh.reshape(b3.shape)
        k3 = k_s[:, sl].reshape(b3.shape)
        blast = b3[:, CHUNK - 1:CHUNK, :]
        bmid = b3[:, MID - 1:MID, :]
        qe = (qh * jnp.exp(bh)).astype(_BF16)
        kd = (k3 * jnp.exp(blast - b3)).reshape(rows, D_HEAD).astype(_BF16)
        qm = (q3 * jnp.exp(b3 - bmid)).reshape(rows, D_HEAD).astype(_BF16)
        km = (k3 * jnp.exp(bmid - b3)).reshape(rows, D_HEAD).astype(_BF16)
        return qe, kd, qm, km, jnp.exp(blast)

    def hg_state_matmuls(hd, factors, pieces):
        qe, kd, qm, km, chunk_dec = factors
        sl = slice(hd * D_HEAD, (hd + 1) * D_HEAD)
        vh = v_s[:, sl]
        states = [shg_ref[0, b, hd] for b in range(nb)]
        amats = [jnp.where(mid_ok, jnp.where(causal, _dot_nt(qm[rs], km[rs]), 0.0), am_s[hd, rs, :])
                 for rs in seq_rows]
        _run(pieces[0])
        outs = [_dot(amats[b].astype(_BF16), vh[rs]) + _dot_nt(qe[rs], states[b].astype(_BF16))
                for b, rs in enumerate(seq_rows)]
        _run(pieces[1])
        for b, rs in enumerate(seq_rows):
            shg_ref[0, b, hd] = states[b] * chunk_dec[b] + _dot_tn(vh[rs], kd[rs])
        oa_s[:, sl] = _rms_rows(jnp.concatenate(outs, axis=0), hgw)

    factors = hg_factors(0)
    for hd in range(N_HEADS):
        _run(hg_pieces[hd][0:1])
        upcoming = hg_factors(hd + 1) if hd + 1 < N_HEADS else None
        hg_state_matmuls(hd, factors, (hg_pieces[hd][1:2], ()))
        factors = upcoming

    pos0 = jnp.where(t == nt, PAST_LEN, t * CHUNK)
    posf = (lax.broadcasted_iota(jnp.int32, (CHUNK, D_HEAD), 0) + pos0).astype(_F32)
    ang = posf * inv_ref[...]
    lane = lax.broadcasted_iota(jnp.int32, (CHUNK, D_HEAD), 1)
    cosf = jnp.cos(ang)
    sinf = jnp.where(lane < D_HEAD // 2, -1.0, 1.0) * jnp.sin(ang)
    cos3 = jnp.broadcast_to(cosf[None], (nb, CHUNK, D_HEAD)).reshape(rows, D_HEAD)
    sin3 = jnp.broadcast_to(sinf[None], (nb, CHUNK, D_HEAD)).reshape(rows, D_HEAD)

    def rope(v):
        return v * cos3 + pltpu.roll(v, D_HEAD // 2, 1) * sin3

    def za_piece(c):
        za_s[:, c:c + PIECE] = _dot(oag_s[...], wbr_ref[0, :, c:c + PIECE])

    def gate_a_piece(c):
        m_s[:, c:c + PIECE] = _sigmoid(proj(8, PIECE, c)) * za_s[:, c:c + PIECE]

    def gate_b_piece(c):
        gb_s[:, c:c + PIECE] = _sigmoid(proj(8, PIECE, dm + c))

    oag_s[...] = (oa_s[...] * sga_s[...]).astype(_BF16)
    ret_list = ([functools.partial(proj_piece, sgr_s, 7, _silu, c) for c in range(0, D_BRANCH, PIECE)]
                + [functools.partial(za_piece, c) for c in range(0, dm, PIECE)]
                + [functools.partial(gate_a_piece, c) for c in range(0, dm, PIECE)]
                + [functools.partial(gate_b_piece, c) for c in range(0, dm, PIECE)])
    per_head = -(-len(ret_list) // N_HEADS)
    ret_pieces = [ret_list[i * per_head:(i + 1) * per_head] for i in range(N_HEADS)]

    idx_r = jrow.astype(_F32)
    rel = (lax.broadcasted_iota(jnp.int32, (CHUNK, CHUNK), 0)
           - lax.broadcasted_iota(jnp.int32, (CHUNK, CHUNK), 1)).astype(_F32)
    rnw = rnw_ref[...]
    rnb = rnb_ref[...]
    log_gamma = [math.log(1.0 - 2.0 ** (-5.0 - hd)) for hd in range(N_HEADS)]

    def ret_factors(hd):
        sl = slice(hd * D_HEAD, (hd + 1) * D_HEAD)
        lg = log_gamma[hd]
        kf = rope(rk_s[:, sl]) * (D_HEAD ** -0.5)
        qf = rope(rq_s[:, sl])
        kdec = (kf * jnp.exp(lg * (CHUNK - 1.0 - idx_r))).astype(_BF16)
        qdec = (qf * jnp.exp(lg * (idx_r + 1.0))).astype(_BF16)
        dmat = jnp.where(rel >= 0, jnp.exp(lg * jnp.maximum(rel, 0.0)), 0.0)
        return qf.astype(_BF16), kf.astype(_BF16), kdec, qdec, dmat

    def ret_state_matmuls(hd, factors, pieces):
        qb, kb, kdec, qdec, dmat = factors
        sl = slice(hd * D_HEAD, (hd + 1) * D_HEAD)
        vh = rv_s[:, sl]
        states = [sret_ref[0, b, hd] for b in range(nb)]
        amats = [_dot_nt(qb[rs], kb[rs]) * dmat for rs in seq_rows]
        _run(pieces[0])
        outs = [_dot(amats[b].astype(_BF16), vh[rs]) + _dot(qdec[rs], states[b].astype(_BF16))
                for b, rs in enumerate(seq_rows)]
        _run(pieces[1])
        for b, rs in enumerate(seq_rows):
            sret_ref[0, b, hd] = math.exp(log_gamma[hd] * CHUNK) * states[b] + _dot_tn(kdec[rs], vh[rs])
        o = jnp.concatenate(outs, axis=0)
        mu = jnp.mean(o, axis=-1, keepdims=True)
        oc = o - mu
        var = jnp.mean(oc * oc, axis=-1, keepdims=True)
        ob_s[:, sl] = oc * lax.rsqrt(var + GN_EPS) * rnw[:, sl] + rnb[:, sl]

    factors = ret_factors(0)
    for hd in range(N_HEADS):
        mine = ret_pieces[hd]
        _run(mine[0:2])
        upcoming = ret_factors(hd + 1) if hd + 1 < N_HEADS else None
        ret_state_matmuls(hd, factors, (mine[2:3], mine[3:]))
        factors = upcoming

    z_b = _dot((ob_s[...] * sgr_s[...]).astype(_BF16), wbr_ref[1])
    merged = m_s[...] + gb_s[...] * z_b
    y = load_x() + _dot(merged.astype(_BF16), wout_ref[...])
    y_ref[...] = y.reshape(y_ref.shape)


def _layer_spec(block):
    zeros = (0,) * (len(block) - 1)
    return pl.BlockSpec(block, lambda t, l: (l[0],) + zeros, pipeline_mode=pl.Buffered(1))


def _const_spec(block):
    zeros = (0,) * len(block)
    return pl.BlockSpec(block, lambda t, l: zeros, pipeline_mode=pl.Buffered(1))


def _mixer_call(nt, lv, xs_in, shg_in, sret_in, norm_w, w_in, lb_logits, hg_norm_w, ret_norm_w,
                ret_norm_b, w_branch, w_out, inv2, tril3, e8):
    first = len(xs_in) == 2
    nb, _, dm = xs_in[0].shape
    depth = w_in.shape[0]
    rows = nb * CHUNK
    d_in = w_in.shape[2]
    st_block = (None, nb, N_HEADS, D_HEAD, D_HEAD)
    x_spec = pl.BlockSpec((nb, CHUNK, dm), lambda t, l: (0, t, 0))
    if first:
        x_specs = [pl.BlockSpec((nb, CHUNK, dm), lambda t, l: (0, jnp.minimum(t, nt - 1), 0)),
                   _const_spec((nb, CHUNK, dm))]
    else:
        x_specs = [x_spec]
    in_specs = x_specs + [
        _layer_spec(st_block),
        _layer_spec(st_block),
        _layer_spec((None, 1, dm)),
        _layer_spec((None, dm, d_in)),
        _const_spec((depth, D_BRANCH)),
        _layer_spec((None, 1, D_HEAD)),
        _layer_spec((None, 1, D_BRANCH)),
        _layer_spec((None, 1, D_BRANCH)),
        _layer_spec((None, 2, D_BRANCH, dm)),
        _layer_spec((None, dm, dm)),
        _const_spec((1, D_HEAD)),
        _const_spec((CHUNK, 3 * CHUNK)),
        _const_spec((SUB * D_HEAD, D_HEAD)),
    ]
    st_out = jax.ShapeDtypeStruct((2, nb, N_HEADS, D_HEAD, D_HEAD), _F32)
    st_out_spec = pl.BlockSpec((1, nb, N_HEADS, D_HEAD, D_HEAD), lambda t, l: (t // nt, 0, 0, 0, 0))
    scratch = [
        pltpu.VMEM((rows, dm), _BF16),
        pltpu.VMEM((rows, D_BRANCH), _F32),
        pltpu.VMEM((rows, D_BRANCH), _F32),
        pltpu.VMEM((rows, D_BRANCH), _F32),
        pltpu.VMEM((rows, D_BRANCH), _BF16),
        pltpu.VMEM((rows, D_BRANCH), _F32),
        pltpu.VMEM((rows, D_BRANCH), _F32),
        pltpu.VMEM((rows, D_BRANCH), _BF16),
        pltpu.VMEM((rows, D_BRANCH), _F32),
        pltpu.VMEM((rows, D_BRANCH), _F32),
        pltpu.VMEM((rows, D_BRANCH), _F32),
        pltpu.VMEM((rows, D_BRANCH), _F32),
        pltpu.VMEM((rows, D_BRANCH), _BF16),
        pltpu.VMEM((rows, dm), _F32),
        pltpu.VMEM((rows, dm), _F32),
        pltpu.VMEM((rows, dm), _F32),
        pltpu.VMEM((N_HEADS, CHUNK, N_SLOT * D_HEAD), _BF16),
        pltpu.VMEM((N_HEADS, CHUNK, N_SLOT * D_HEAD), _BF16),
        pltpu.VMEM((N_HEADS, CHUNK, SUB * D_HEAD), _BF16),
        pltpu.VMEM((N_HEADS, rows, CHUNK), _F32),
    ]
    grid_spec = pltpu.PrefetchScalarGridSpec(
        num_scalar_prefetch=1, grid=(nt + 1,), in_specs=in_specs,
        out_specs=[x_spec, st_out_spec, st_out_spec], scratch_shapes=scratch)
    return pl.pallas_call(
        functools.partial(_mixer_kernel, nt, first),
        grid_spec=grid_spec,
        out_shape=[jax.ShapeDtypeStruct((nb, (nt + 1) * CHUNK, dm), _F32), st_out, st_out],
        input_output_aliases={} if first else {1: 0},
        compiler_params=pltpu.CompilerParams(dimension_semantics=("arbitrary",),
                                             vmem_limit_bytes=VMEM_LIMIT_BYTES),
        name="mixer_first" if first else "mixer",
    )(lv, *xs_in, shg_in, sret_in, norm_w, w_in, lb_logits, hg_norm_w, ret_norm_w, ret_norm_b, w_branch,
      w_out, inv2, tril3, e8)


def _ffn_kernel(nt, last, l_ref, x_ref, cc_in_ref, nw_ref, wup_ref, cw_ref, cb_ref, wd_ref, nfw_ref, *refs):
    if last:
        yp_ref, ys_ref, cc_ref, h_s, acc_s, ext_s, g_s = refs
    else:
        y_ref, cc_ref, h_s, acc_s, ext_s, g_s = refs
    t = pl.program_id(0)
    nb, _, dm = x_ref.shape
    rows = nb * CHUNK
    d_ff = wd_ref.shape[0]
    fb = FFN_BLOCK
    nblk = d_ff // fb

    @pl.when(t == 0)
    def _():
        cc_ref[...] = jnp.zeros(cc_ref.shape, _F32)

    @pl.when(t == nt)
    def _():
        cc_ref[0] = cc_in_ref[...]

    h_s[...] = _rms_rows(x_ref[...].reshape(rows, dm), nw_ref[...]).astype(_BF16)

    def up(i):
        h = h_s[...]
        cols = slice(i * fb, (i + 1) * fb)
        gcols = slice(d_ff + i * fb, d_ff + (i + 1) * fb)
        ext_s[i % 2, :, 8:8 + CHUNK, :] = _dot(h, wup_ref[:, cols]).reshape(nb, CHUNK, fb)
        g_s[i % 2] = _dot(h, wup_ref[:, gcols])

    def down(i):
        cols = slice(i * fb, (i + 1) * fb)
        ext = ext_s.at[i % 2]
        ext[:, 6:8, :] = cc_ref[0, :, :, cols]
        a3 = ext[:, 8:8 + CHUNK, :]
        cw = cw_ref[:, cols]
        conv = (cb_ref[:, cols] + ext[:, 6:6 + CHUNK, :] * cw[0:1, :] + ext[:, 7:7 + CHUNK, :] * cw[1:2, :]
                + a3 * cw[2:3, :])
        cc_ref[0, :, :, cols] = a3[:, CHUNK - 2:CHUNK, :]
        u = (_silu(conv).reshape(rows, fb) * g_s[i % 2]).astype(_BF16)
        return _dot(u, wd_ref[cols, :])

    up(0)
    y = None
    for i in range(nblk):
        if i + 1 < nblk:
            up(i + 1)
        part = down(i)
        prev = x_ref[...].reshape(rows, dm) if i == 0 else acc_s[...]
        if i + 1 < nblk:
            acc_s[...] = prev + part
        else:
            y = prev + part

    if last:
        yn = _rms_rows(y, nfw_ref[...]).reshape(nb, CHUNK, dm)

        @pl.when(t < nt)
        def _():
            yp_ref[...] = yn

        @pl.when(t == nt)
        def _():
            ys_ref[...] = yn
    else:
        y_ref[...] = y.reshape(y_ref.shape)


def _ffn_call(nt, last, lv, x, cc_in, norm_w, w_up, cw, cb, wd, nfw):
    nb, _, dm = x.shape
    rows = nb * CHUNK
    d_ff = wd.shape[1]
    x_spec = pl.BlockSpec((nb, CHUNK, dm), lambda t, l: (0, t, 0))
    in_specs = [
        x_spec,
        _layer_spec((None, nb, 2, d_ff)),
        _layer_spec((None, 1, dm)),
        _layer_spec((None, dm, 2 * d_ff)),
        _layer_spec((None, 3, d_ff)),
        _layer_spec((None, 1, d_ff)),
        _layer_spec((None, d_ff, dm)),
        _const_spec((1, dm)),
    ]
    cc_out = jax.ShapeDtypeStruct((2, nb, 2, d_ff), _F32)
    cc_spec = pl.BlockSpec((1, nb, 2, d_ff), lambda t, l: (t // nt, 0, 0, 0))
    if last:
        out_shape = [jax.ShapeDtypeStruct((nb, nt * CHUNK, dm), _F32), jax.ShapeDtypeStruct((nb, CHUNK, dm), _F32), cc_out]
        out_specs = [pl.BlockSpec((nb, CHUNK, dm), lambda t, l: (0, jnp.minimum(t, nt - 1), 0)),
                     pl.BlockSpec((nb, CHUNK, dm), lambda t, l: (0, 0, 0)), cc_spec]
    else:
        out_shape = [jax.ShapeDtypeStruct(x.shape, _F32), cc_out]
        out_specs = [x_spec, cc_spec]
    scratch = [
        pltpu.VMEM((rows, dm), _BF16),
        pltpu.VMEM((rows, dm), _F32),
        pltpu.VMEM((2, nb, CHUNK + 8, FFN_BLOCK), _F32),
        pltpu.VMEM((2, rows, FFN_BLOCK), _F32),
    ]
    grid_spec = pltpu.PrefetchScalarGridSpec(
        num_scalar_prefetch=1, grid=(nt + 1,), in_specs=in_specs, out_specs=out_specs, scratch_shapes=scratch)
    return pl.pallas_call(
        functools.partial(_ffn_kernel, nt, last),
        grid_spec=grid_spec,
        out_shape=out_shape,
        input_output_aliases={} if last else {1: 0},
        compiler_params=pltpu.CompilerParams(dimension_semantics=("arbitrary",),
                                             vmem_limit_bytes=VMEM_LIMIT_BYTES),
        name="ffn_last" if last else "ffn",
    )(lv, x, cc_in, norm_w, w_up, cw, cb, wd, nfw)


def _constants():
    half = D_HEAD // 2
    inv = ROPE_BASE ** (-jnp.arange(half, dtype=_F32) / half)
    inv2 = jnp.concatenate([inv, inv])[None, :]
    r = np.arange(CHUNK)
    tril = (r[:, None] >= r[None, :]).astype(np.float32)
    tril3 = jnp.asarray(np.concatenate([tril, tril, tril], axis=1), _BF16)
    e8 = np.zeros((SUB * D_HEAD, D_HEAD), np.float32)
    cols = np.arange(CHUNK)
    for s in range(SUB):
        e8[s * D_HEAD:(s + 1) * D_HEAD, cols[cols % SUB == s]] = 1.0
    return inv2, tril3, jnp.asarray(e8, _BF16)


def kernel(x_prompt, x_sample, state_hgrn, state_ret, cache_conv, norm_mix_w, w_in, hg_lb_logits,
           hg_norm_w, ret_norm_w, ret_norm_b, w_branch, w_out, norm_ffn_w, w_up, conv_w, conv_b,
           w_down, norm_final_w):
    nb, tp, dm = x_prompt.shape
    depth = w_in.shape[0]
    d_ff = w_down.shape[1]
    assert x_sample.shape == (nb, CHUNK, dm) and tp % CHUNK == 0 and d_ff % FFN_BLOCK == 0
    nt = tp // CHUNK
    inv2, tril3, e8 = _constants()

    w_in_b = w_in.astype(_BF16)
    w_branch_b = w_branch.astype(_BF16)
    w_out_b = w_out.astype(_BF16)
    w_up_b = w_up.astype(_BF16)
    w_down_b = w_down.astype(_BF16)
    shg_in = jnp.swapaxes(state_hgrn, -1, -2)
    nmw = norm_mix_w[:, None, :]
    nfw_l = norm_ffn_w[:, None, :]
    hgw = hg_norm_w[:, None, :]
    rnw = ret_norm_w[:, None, :]
    rnb = ret_norm_b[:, None, :]
    cb = conv_b[:, None, :]
    nfin = norm_final_w[None, :]

    xs = (x_prompt, x_sample)
    hg_all, ret_all, cc_all = [], [], []
    for l in range(depth):
        lv = jnp.full((1,), l, jnp.int32)
        x, s_hg, s_ret = _mixer_call(nt, lv, xs, shg_in, state_ret, nmw, w_in_b, hg_lb_logits, hgw, rnw, rnb,
                                     w_branch_b, w_out_b, inv2, tril3, e8)
        *xs, s_cc = _ffn_call(nt, l == depth - 1, lv, x, cache_conv, nfw_l, w_up_b, conv_w, cb, w_down_b, nfin)
        hg_all.append(s_hg)
        ret_all.append(s_ret)
        cc_all.append(s_cc)
    y_prompt, y_sample = xs
    hg_all = jnp.swapaxes(jnp.stack(hg_all), -1, -2)
    ret_all = jnp.stack(ret_all)
    cc_all = jnp.stack(cc_all)
    return (y_prompt, y_sample, hg_all[:, 0], ret_all[:, 0], cc_all[:, 0],
            hg_all[:, 1], ret_all[:, 1], cc_all[:, 1])
```

```python
import functools
import math

import numpy as np
import jax
import jax.numpy as jnp
from jax import lax
from jax.experimental import pallas as pl
from jax.experimental.pallas import tpu as pltpu

CHUNK = 64
SUB = 8
LEVELS = (64, 32, 16)
N_SLOT = sum(CHUNK // m for m in LEVELS)
N_HEADS = 4
D_HEAD = 128
D_BRANCH = N_HEADS * D_HEAD
PAST_LEN = 2048
ROPE_BASE = 10000.0
NORM_EPS = 1e-6
GN_EPS = 1e-5
F_FLOOR = 1e-30
MID = CHUNK // 2
MID_SPREAD_MAX = 80.0
FFN_BLOCK = 256
PIECE = 256
VMEM_LIMIT_BYTES = 60 * 1024 * 1024

_F32 = jnp.float32
_BF16 = jnp.bfloat16


def _dot(a, b):
    return jnp.dot(a, b, preferred_element_type=_F32)


def _dot_nt(a, b):
    return lax.dot_general(a, b, (((1,), (1,)), ((), ())), preferred_element_type=_F32)


def _dot_tn(a, b):
    return lax.dot_general(a, b, (((0,), (0,)), ((), ())), preferred_element_type=_F32)


def _sigmoid(x):
    return jax.nn.sigmoid(x)


def _silu(x):
    return x * jax.nn.sigmoid(x)


def _run(thunks):
    for thunk in thunks:
        thunk()


def _rms_rows(x, w):
    ms = jnp.mean(x * x, axis=-1, keepdims=True)
    return x * lax.rsqrt(ms + NORM_EPS) * w


def _factored_scores(qh, kh, bh, cat1, cat2, p_buf, e8_ref):
    rows = qh.shape[0]
    jrow = lax.broadcasted_iota(jnp.int32, (rows, D_HEAD), 0)
    slot = 0
    for m in LEVELS:
        half = m // 2
        bm = bh.reshape(rows // m, m, D_HEAD)
        delta = bm - bm[:, half - 1:half, :]
        is_q = lax.broadcasted_iota(jnp.int32, bm.shape, 1) >= half
        dec = jnp.exp(jnp.minimum(jnp.where(is_q, delta, -delta), 0.0))
        xm = (jnp.where(is_q, qh.reshape(bm.shape), kh.reshape(bm.shape)) * dec).reshape(rows, D_HEAD)
        hidx = jrow >> int(math.log2(half))
        for s in range(rows // m):
            cs = slice(slot * D_HEAD, (slot + 1) * D_HEAD)
            cat1[:, cs] = jnp.where(hidx == 2 * s + 1, xm, 0.0).astype(_BF16)
            cat2[:, cs] = jnp.where(hidx == 2 * s, xm, 0.0).astype(_BF16)
            slot += 1
    rowg = lax.broadcasted_iota(jnp.int32, (rows // SUB, SUB, D_HEAD), 1)
    qg = qh.reshape(rowg.shape)
    kg = kh.reshape(rowg.shape)
    bg = bh.reshape(rowg.shape)
    for s in range(SUB):
        msk = rowg >= s
        dec = jnp.exp(jnp.where(msk, bg - bg[:, s:s + 1, :], 0.0))
        ps = jnp.where(msk, dec * qg * kg[:, s:s + 1, :], 0.0)
        p_buf[:, s * D_HEAD:(s + 1) * D_HEAD] = ps.reshape(rows, D_HEAD).astype(_BF16)
    ad_col = lax.broadcasted_iota(jnp.int32, (rows, D_HEAD), 1)
    ad = jnp.where((ad_col >> 3) == (jrow >> 3), _dot(p_buf[...], e8_ref[...]), 0.0)
    return _dot_nt(cat1[...], cat2[...]) + ad[:, 0:CHUNK]


def _mixer_kernel(nt, first, l_ref, *refs):
    if first:
        xp_ref, xs_ref, *refs = refs
    else:
        x_ref, *refs = refs
    (shg_in_ref, sret_in_ref, nw_ref, win_ref, lbl_ref, hgw_ref, rnw_ref, rnb_ref, wbr_ref, wout_ref, inv_ref,
     tril_ref, e8_ref,
     y_ref, shg_ref, sret_ref,
     h_s, q_s, k_s, bc_s, v_s, rq_s, rk_s, rv_s, sga_s, sgr_s, oa_s, ob_s, oag_s, za_s, m_s, gb_s,
     cat1_s, cat2_s, p_s, am_s) = refs
    t = pl.program_id(0)
    layer = l_ref[0]
    nb, _, dm = y_ref.shape
    rows = nb * CHUNK
    seq_rows = [slice(b * CHUNK, (b + 1) * CHUNK) for b in range(nb)]

    def load_x():
        if first:
            return jnp.where(t == nt, xs_ref[...], xp_ref[...]).reshape(rows, dm)
        return x_ref[...].reshape(rows, dm)

    @pl.when(t == 0)
    def _():
        shg_ref[...] = jnp.zeros(shg_ref.shape, _F32)
        sret_ref[...] = jnp.zeros(sret_ref.shape, _F32)

    @pl.when(t == nt)
    def _():
        shg_ref[0] = shg_in_ref[...]
        sret_ref[0] = sret_in_ref[...]

    h_s[...] = _rms_rows(load_x(), nw_ref[...]).astype(_BF16)

    def proj(group, width=D_BRANCH, offset=0):
        c0 = group * D_BRANCH + offset
        return _dot(h_s[...], win_ref[:, c0:c0 + width])

    logits = lbl_ref[...]
    ex = jnp.exp(logits - jnp.max(logits, axis=0, keepdims=True))
    prob = ex / jnp.sum(ex, axis=0, keepdims=True)
    lrow = lax.broadcasted_iota(jnp.int32, prob.shape, 0)
    lb = jnp.sum(jnp.where((lrow >= 1) & (lrow <= layer), prob, 0.0), axis=0, keepdims=True)

    f_gate = lb + (1.0 - lb) * _sigmoid(proj(1))
    q_s[...] = _silu(proj(0))
    for c in range(0, D_BRANCH, PIECE):
        rq_s[:, c:c + PIECE] = proj(4, PIECE, c)
    k_s[...] = 1.0 - f_gate
    log_f = jnp.log(jnp.maximum(f_gate, F_FLOOR))
    lf_hi = log_f.astype(_BF16)
    r1 = log_f - lf_hi.astype(_F32)
    lf_mid = r1.astype(_BF16)
    lf_lo = (r1 - lf_mid.astype(_F32)).astype(_BF16)
    tril3 = tril_ref[...]
    for rs in seq_rows:
        bc_s[rs, :] = _dot(tril3, jnp.concatenate([lf_hi[rs], lf_mid[rs], lf_lo[rs]], axis=0))
    v_s[...] = proj(2).astype(_BF16)

    jrow = lax.broadcasted_iota(jnp.int32, (rows, D_HEAD), 0) & (CHUNK - 1)
    hgw = hgw_ref[...]

    ball = bc_s[...].reshape(nb, CHUNK, D_BRANCH)
    spread = jnp.max(jnp.abs(ball - ball[:, MID - 1:MID, :]))
    mid_ok = spread <= MID_SPREAD_MAX

    @pl.when(t == 0)
    def _():
        am_s[...] = jnp.zeros(am_s.shape, _F32)

    @pl.when(jnp.logical_not(mid_ok))
    def _():
        def seq_body(b, carry):
            rs = pl.ds(pl.multiple_of(b * CHUNK, CHUNK), CHUNK)
            for hd in range(N_HEADS):
                sl = slice(hd * D_HEAD, (hd + 1) * D_HEAD)
                am_s[hd, rs, :] = _factored_scores(q_s[rs, sl], k_s[rs, sl], bc_s[rs, sl], cat1_s.at[hd],
                                                   cat2_s.at[hd], p_s.at[hd], e8_ref)
            return carry

        lax.fori_loop(0, nb, seq_body, 0)

    causal = (lax.broadcasted_iota(jnp.int32, (CHUNK, CHUNK), 0)
              >= lax.broadcasted_iota(jnp.int32, (CHUNK, CHUNK), 1))

    def proj_piece(dst, group, post, c):
        dst[:, c:c + PIECE] = post(proj(group, PIECE, c))

    def to_bf16(v):
        return v.astype(_BF16)

    def identity(v):
        return v

    def gate_b_piece(c):
        gb_s[:, c:c + PIECE] = _sigmoid(proj(8, PIECE, dm + c))

    hg_units = ((rk_s, 5, identity), (rv_s, 6, to_bf16), (sga_s, 3, _silu), (sgr_s, 7, _silu))
    hg_list = ([functools.partial(proj_piece, dst, group, post, c) for dst, group, post in hg_units
                for c in range(0, D_BRANCH, PIECE)]
               + [functools.partial(gate_b_piece, c) for c in range(0, dm, PIECE)])
    per_hg = -(-len(hg_list) // N_HEADS)
    hg_pieces = [hg_list[i * per_hg:(i + 1) * per_hg] for i in range(N_HEADS)]

    def hg_factors(hd):
        sl = slice(hd * D_HEAD, (hd + 1) * D_HEAD)
        qh = q_s[:, sl]
        bh = bc_s[:, sl]
        b3 = bh.reshape(nb, CHUNK, D_HEAD)
        q3 = qh.reshape(b3.shape)
        k3 = k_s[:, sl].reshape(b3.shape)
        blast = b3[:, CHUNK - 1:CHUNK, :]
        bmid = b3[:, MID - 1:MID, :]
        qe = (qh * jnp.exp(bh)).astype(_BF16)
        kd = (k3 * jnp.exp(blast - b3)).reshape(rows, D_HEAD).astype(_BF16)
        qm = (q3 * jnp.exp(b3 - bmid)).reshape(rows, D_HEAD).astype(_BF16)
        km = (k3 * jnp.exp(bmid - b3)).reshape(rows, D_HEAD).astype(_BF16)
        return qe, kd, qm, km, jnp.exp(blast)

    def hg_state_matmuls(hd, factors, pieces):
        qe, kd, qm, km, chunk_dec = factors
        sl = slice(hd * D_HEAD, (hd + 1) * D_HEAD)
        vh = v_s[:, sl]
        states = [shg_ref[0, b, hd] for b in range(nb)]
        amats = [jnp.where(mid_ok, jnp.where(causal, _dot_nt(qm[rs], km[rs]), 0.0), am_s[hd, rs, :])
                 for rs in seq_rows]
        _run(pieces[0])
        outs = [_dot(amats[b].astype(_BF16), vh[rs]) + _dot_nt(qe[rs], states[b].astype(_BF16))
                for b, rs in enumerate(seq_rows)]
        _run(pieces[1])
        for b, rs in enumerate(seq_rows):
            shg_ref[0, b, hd] = states[b] * chunk_dec[b] + _dot_tn(vh[rs], kd[rs])
        oa_s[:, sl] = _rms_rows(jnp.concatenate(outs, axis=0), hgw)

    factors = hg_factors(0)
    for hd in range(N_HEADS):
        mine = hg_pieces[hd]
        _run(mine[0:1])
        upcoming = hg_factors(hd + 1) if hd + 1 < N_HEADS else None
        hg_state_matmuls(hd, factors, (mine[1:2], mine[2:]))
        factors = upcoming

    pos0 = jnp.where(t == nt, PAST_LEN, t * CHUNK)
    posf = (lax.broadcasted_iota(jnp.int32, (CHUNK, D_HEAD), 0) + pos0).astype(_F32)
    ang = posf * inv_ref[...]
    lane = lax.broadcasted_iota(jnp.int32, (CHUNK, D_HEAD), 1)
    cosf = jnp.cos(ang)
    sinf = jnp.where(lane < D_HEAD // 2, -1.0, 1.0) * jnp.sin(ang)
    cos3 = jnp.broadcast_to(cosf[None], (nb, CHUNK, D_HEAD)).reshape(rows, D_HEAD)
    sin3 = jnp.broadcast_to(sinf[None], (nb, CHUNK, D_HEAD)).reshape(rows, D_HEAD)

    def rope(v):
        return v * cos3 + pltpu.roll(v, D_HEAD // 2, 1) * sin3

    def za_piece(c):
        za_s[:, c:c + PIECE] = _dot(oag_s[...], wbr_ref[0, :, c:c + PIECE])

    def gate_a_piece(c):
        m_s[:, c:c + PIECE] = _sigmoid(proj(8, PIECE, c)) * za_s[:, c:c + PIECE]

    oag_s[...] = (oa_s[...] * sga_s[...]).astype(_BF16)
    ret_list = ([functools.partial(za_piece, c) for c in range(0, dm, PIECE)]
                + [functools.partial(gate_a_piece, c) for c in range(0, dm, PIECE)])
    per_head = -(-len(ret_list) // N_HEADS)
    ret_pieces = [ret_list[i * per_head:(i + 1) * per_head] for i in range(N_HEADS)]

    idx_r = jrow.astype(_F32)
    rel = (lax.broadcasted_iota(jnp.int32, (CHUNK, CHUNK), 0)
           - lax.broadcasted_iota(jnp.int32, (CHUNK, CHUNK), 1)).astype(_F32)
    rnw = rnw_ref[...]
    rnb = rnb_ref[...]
    log_gamma = [math.log(1.0 - 2.0 ** (-5.0 - hd)) for hd in range(N_HEADS)]

    def ret_factors(hd):
        sl = slice(hd * D_HEAD, (hd + 1) * D_HEAD)
        lg = log_gamma[hd]
        kf = rope(rk_s[:, sl]) * (D_HEAD ** -0.5)
        qf = rope(rq_s[:, sl])
        kdec = (kf * jnp.exp(lg * (CHUNK - 1.0 - idx_r))).astype(_BF16)
        qdec = (qf * jnp.exp(lg * (idx_r + 1.0))).astype(_BF16)
        dmat = jnp.where(rel >= 0, jnp.exp(lg * jnp.maximum(rel, 0.0)), 0.0)
        return qf.astype(_BF16), kf.astype(_BF16), kdec, qdec, dmat

    def ret_state_matmuls(hd, factors, pieces):
        qb, kb, kdec, qdec, dmat = factors
        sl = slice(hd * D_HEAD, (hd + 1) * D_HEAD)
        vh = rv_s[:, sl]
        states = [sret_ref[0, b, hd] for b in range(nb)]
        amats = [_dot_nt(qb[rs], kb[rs]) * dmat for rs in seq_rows]
        _run(pieces[0])
        outs = [_dot(amats[b].astype(_BF16), vh[rs]) + _dot(qdec[rs], states[b].astype(_BF16))
                for b, rs in enumerate(seq_rows)]
        _run(pieces[1])
        for b, rs in enumerate(seq_rows):
            sret_ref[0, b, hd] = math.exp(log_gamma[hd] * CHUNK) * states[b] + _dot_tn(kdec[rs], vh[rs])
        o = jnp.concatenate(outs, axis=0)
        mu = jnp.mean(o, axis=-1, keepdims=True)
        oc = o - mu
        var = jnp.mean(oc * oc, axis=-1, keepdims=True)
        ob_s[:, sl] = oc * lax.rsqrt(var + GN_EPS) * rnw[:, sl] + rnb[:, sl]

    factors = ret_factors(0)
    for hd in range(N_HEADS):
        mine = ret_pieces[hd]
        _run(mine[0:1])
        upcoming = ret_factors(hd + 1) if hd + 1 < N_HEADS else None
        ret_state_matmuls(hd, factors, (mine[1:2], mine[2:]))
        factors = upcoming

    z_b = _dot((ob_s[...] * sgr_s[...]).astype(_BF16), wbr_ref[1])
    merged = m_s[...] + gb_s[...] * z_b
    y = load_x() + _dot(merged.astype(_BF16), wout_ref[...])
    y_ref[...] = y.reshape(y_ref.shape)


def _layer_spec(block):
    zeros = (0,) * (len(block) - 1)
    return pl.BlockSpec(block, lambda t, l: (l[0],) + zeros, pipeline_mode=pl.Buffered(1))


def _const_spec(block):
    zeros = (0,) * len(block)
    return pl.BlockSpec(block, lambda t, l: zeros, pipeline_mode=pl.Buffered(1))


def _mixer_call(nt, lv, xs_in, shg_in, sret_in, norm_w, w_in, lb_logits, hg_norm_w, ret_norm_w,
                ret_norm_b, w_branch, w_out, inv2, tril3, e8):
    first = len(xs_in) == 2
    nb, _, dm = xs_in[0].shape
    depth = w_in.shape[0]
    rows = nb * CHUNK
    d_in = w_in.shape[2]
    st_block = (None, nb, N_HEADS, D_HEAD, D_HEAD)
    x_spec = pl.BlockSpec((nb, CHUNK, dm), lambda t, l: (0, t, 0))
    if first:
        x_specs = [pl.BlockSpec((nb, CHUNK, dm), lambda t, l: (0, jnp.minimum(t, nt - 1), 0)),
                   _const_spec((nb, CHUNK, dm))]
    else:
        x_specs = [x_spec]
    in_specs = x_specs + [
        _layer_spec(st_block),
        _layer_spec(st_block),
        _layer_spec((None, 1, dm)),
        _layer_spec((None, dm, d_in)),
        _const_spec((depth, D_BRANCH)),
        _layer_spec((None, 1, D_HEAD)),
        _layer_spec((None, 1, D_BRANCH)),
        _layer_spec((None, 1, D_BRANCH)),
        _layer_spec((None, 2, D_BRANCH, dm)),
        _layer_spec((None, dm, dm)),
        _const_spec((1, D_HEAD)),
        _const_spec((CHUNK, 3 * CHUNK)),
        _const_spec((SUB * D_HEAD, D_HEAD)),
    ]
    st_out = jax.ShapeDtypeStruct((2, nb, N_HEADS, D_HEAD, D_HEAD), _F32)
    st_out_spec = pl.BlockSpec((1, nb, N_HEADS, D_HEAD, D_HEAD), lambda t, l: (t // nt, 0, 0, 0, 0))
    scratch = [
        pltpu.VMEM((rows, dm), _BF16),
        pltpu.VMEM((rows, D_BRANCH), _F32),
        pltpu.VMEM((rows, D_BRANCH), _F32),
        pltpu.VMEM((rows, D_BRANCH), _F32),
        pltpu.VMEM((rows, D_BRANCH), _BF16),
        pltpu.VMEM((rows, D_BRANCH), _F32),
        pltpu.VMEM((rows, D_BRANCH), _F32),
        pltpu.VMEM((rows, D_BRANCH), _BF16),
        pltpu.VMEM((rows, D_BRANCH), _F32),
        pltpu.VMEM((rows, D_BRANCH), _F32),
        pltpu.VMEM((rows, D_BRANCH), _F32),
        pltpu.VMEM((rows, D_BRANCH), _F32),
        pltpu.VMEM((rows, D_BRANCH), _BF16),
        pltpu.VMEM((rows, dm), _F32),
        pltpu.VMEM((rows, dm), _F32),
        pltpu.VMEM((rows, dm), _F32),
        pltpu.VMEM((N_HEADS, CHUNK, N_SLOT * D_HEAD), _BF16),
        pltpu.VMEM((N_HEADS, CHUNK, N_SLOT * D_HEAD), _BF16),
        pltpu.VMEM((N_HEADS, CHUNK, SUB * D_HEAD), _BF16),
        pltpu.VMEM((N_HEADS, rows, CHUNK), _F32),
    ]
    grid_spec = pltpu.PrefetchScalarGridSpec(
        num_scalar_prefetch=1, grid=(nt + 1,), in_specs=in_specs,
        out_specs=[x_spec, st_out_spec, st_out_spec], scratch_shapes=scratch)
    return pl.pallas_call(
        functools.partial(_mixer_kernel, nt, first),
        grid_spec=grid_spec,
        out_shape=[jax.ShapeDtypeStruct((nb, (nt + 1) * CHUNK, dm), _F32), st_out, st_out],
        input_output_aliases={} if first else {1: 0},
        compiler_params=pltpu.CompilerParams(dimension_semantics=("arbitrary",),
                                             vmem_limit_bytes=VMEM_LIMIT_BYTES),
        name="mixer_first" if first else "mixer",
    )(lv, *xs_in, shg_in, sret_in, norm_w, w_in, lb_logits, hg_norm_w, ret_norm_w, ret_norm_b, w_branch,
      w_out, inv2, tril3, e8)


def _ffn_kernel(nt, last, l_ref, x_ref, cc_in_ref, nw_ref, wup_ref, cw_ref, cb_ref, wd_ref, nfw_ref, *refs):
    if last:
        yp_ref, ys_ref, cc_ref, h_s, acc_s, ext_s, g_s = refs
    else:
        y_ref, cc_ref, h_s, acc_s, ext_s, g_s = refs
    t = pl.program_id(0)
    nb, _, dm = x_ref.shape
    rows = nb * CHUNK
    d_ff = wd_ref.shape[0]
    fb = FFN_BLOCK
    nblk = d_ff // fb

    @pl.when(t == 0)
    def _():
        cc_ref[...] = jnp.zeros(cc_ref.shape, _F32)

    @pl.when(t == nt)
    def _():
        cc_ref[0] = cc_in_ref[...]

    h_s[...] = _rms_rows(x_ref[...].reshape(rows, dm), nw_ref[...]).astype(_BF16)

    def up(i):
        h = h_s[...]
        cols = slice(i * fb, (i + 1) * fb)
        gcols = slice(d_ff + i * fb, d_ff + (i + 1) * fb)
        ext_s[i % 2, :, 8:8 + CHUNK, :] = _dot(h, wup_ref[:, cols]).reshape(nb, CHUNK, fb)
        g_s[i % 2] = _dot(h, wup_ref[:, gcols])

    def down(i):
        cols = slice(i * fb, (i + 1) * fb)
        ext = ext_s.at[i % 2]
        ext[:, 6:8, :] = cc_ref[0, :, :, cols]
        a3 = ext[:, 8:8 + CHUNK, :]
        cw = cw_ref[:, cols]
        conv = (cb_ref[:, cols] + ext[:, 6:6 + CHUNK, :] * cw[0:1, :] + ext[:, 7:7 + CHUNK, :] * cw[1:2, :]
                + a3 * cw[2:3, :])
        cc_ref[0, :, :, cols] = a3[:, CHUNK - 2:CHUNK, :]
        u = (_silu(conv).reshape(rows, fb) * g_s[i % 2]).astype(_BF16)
        return _dot(u, wd_ref[cols, :])

    up(0)
    y = None
    for i in range(nblk):
        if i + 1 < nblk:
            up(i + 1)
        part = down(i)
        prev = x_ref[...].reshape(rows, dm) if i == 0 else acc_s[...]
        if i + 1 < nblk:
            acc_s[...] = prev + part
        else:
            y = prev + part

    if last:
        yn = _rms_rows(y, nfw_ref[...]).reshape(nb, CHUNK, dm)

        @pl.when(t < nt)
        def _():
            yp_ref[...] = yn

        @pl.when(t == nt)
        def _():
            ys_ref[...] = yn
    else:
        y_ref[...] = y.reshape(y_ref.shape)


def _ffn_call(nt, last, lv, x, cc_in, norm_w, w_up, cw, cb, wd, nfw):
    nb, _, dm = x.shape
    rows = nb * CHUNK
    d_ff = wd.shape[1]
    x_spec = pl.BlockSpec((nb, CHUNK, dm), lambda t, l: (0, t, 0))
    in_specs = [
        x_spec,
        _layer_spec((None, nb, 2, d_ff)),
        _layer_spec((None, 1, dm)),
        _layer_spec((None, dm, 2 * d_ff)),
        _layer_spec((None, 3, d_ff)),
        _layer_spec((None, 1, d_ff)),
        _layer_spec((None, d_ff, dm)),
        _const_spec((1, dm)),
    ]
    cc_out = jax.ShapeDtypeStruct((2, nb, 2, d_ff), _F32)
    cc_spec = pl.BlockSpec((1, nb, 2, d_ff), lambda t, l: (t // nt, 0, 0, 0))
    if last:
        out_shape = [jax.ShapeDtypeStruct((nb, nt * CHUNK, dm), _F32), jax.ShapeDtypeStruct((nb, CHUNK, dm), _F32), cc_out]
        out_specs = [pl.BlockSpec((nb, CHUNK, dm), lambda t, l: (0, jnp.minimum(t, nt - 1), 0)),
                     pl.BlockSpec((nb, CHUNK, dm), lambda t, l: (0, 0, 0)), cc_spec]
    else:
        out_shape = [jax.ShapeDtypeStruct(x.shape, _F32), cc_out]
        out_specs = [x_spec, cc_spec]
    scratch = [
        pltpu.VMEM((rows, dm), _BF16),
        pltpu.VMEM((rows, dm), _F32),
        pltpu.VMEM((2, nb, CHUNK + 8, FFN_BLOCK), _F32),
        pltpu.VMEM((2, rows, FFN_BLOCK), _F32),
    ]
    grid_spec = pltpu.PrefetchScalarGridSpec(
        num_scalar_prefetch=1, grid=(nt + 1,), in_specs=in_specs, out_specs=out_specs, scratch_shapes=scratch)
    return pl.pallas_call(
        functools.partial(_ffn_kernel, nt, last),
        grid_spec=grid_spec,
        out_shape=out_shape,
        input_output_aliases={} if last else {1: 0},
        compiler_params=pltpu.CompilerParams(dimension_semantics=("arbitrary",),
                                             vmem_limit_bytes=VMEM_LIMIT_BYTES),
        name="ffn_last" if last else "ffn",
    )(lv, x, cc_in, norm_w, w_up, cw, cb, wd, nfw)


def _constants():
    half = D_HEAD // 2
    inv = ROPE_BASE ** (-jnp.arange(half, dtype=_F32) / half)
    inv2 = jnp.concatenate([inv, inv])[None, :]
    r = np.arange(CHUNK)
    tril = (r[:, None] >= r[None, :]).astype(np.float32)
    tril3 = jnp.asarray(np.concatenate([tril, tril, tril], axis=1), _BF16)
    e8 = np.zeros((SUB * D_HEAD, D_HEAD), np.float32)
    cols = np.arange(CHUNK)
    for s in range(SUB):
        e8[s * D_HEAD:(s + 1) * D_HEAD, cols[cols % SUB == s]] = 1.0
    return inv2, tril3, jnp.asarray(e8, _BF16)


def kernel(x_prompt, x_sample, state_hgrn, state_ret, cache_conv, norm_mix_w, w_in, hg_lb_logits,
           hg_norm_w, ret_norm_w, ret_norm_b, w_branch, w_out, norm_ffn_w, w_up, conv_w, conv_b,
           w_down, norm_final_w):
    nb, tp, dm = x_prompt.shape
    depth = w_in.shape[0]
    d_ff = w_down.shape[1]
    assert x_sample.shape == (nb, CHUNK, dm) and tp % CHUNK == 0 and d_ff % FFN_BLOCK == 0
    nt = tp // CHUNK
    inv2, tril3, e8 = _constants()

    w_in_b = w_in.astype(_BF16)
    w_branch_b = w_branch.astype(_BF16)
    w_out_b = w_out.astype(_BF16)
    w_up_b = w_up.astype(_BF16)
    w_down_b = w_down.astype(_BF16)
    shg_in = jnp.swapaxes(state_hgrn, -1, -2)
    nmw = norm_mix_w[:, None, :]
    nfw_l = norm_ffn_w[:, None, :]
    hgw = hg_norm_w[:, None, :]
    rnw = ret_norm_w[:, None, :]
    rnb = ret_norm_b[:, None, :]
    cb = conv_b[:, None, :]
    nfin = norm_final_w[None, :]

    xs = (x_prompt, x_sample)
    hg_all, ret_all, cc_all = [], [], []
    for l in range(depth):
        lv = jnp.full((1,), l, jnp.int32)
        x, s_hg, s_ret = _mixer_call(nt, lv, xs, shg_in, state_ret, nmw, w_in_b, hg_lb_logits, hgw, rnw, rnb,
                                     w_branch_b, w_out_b, inv2, tril3, e8)
        *xs, s_cc = _ffn_call(nt, l == depth - 1, lv, x, cache_conv, nfw_l, w_up_b, conv_w, cb, w_down_b, nfin)
        hg_all.append(s_hg)
        ret_all.append(s_ret)
        cc_all.append(s_cc)
    y_prompt, y_sample = xs
    hg_all = jnp.swapaxes(jnp.stack(hg_all), -1, -2)
    ret_all = jnp.stack(ret_all)
    cc_all = jnp.stack(cc_all)
    return (y_prompt, y_sample, hg_all[:, 0], ret_all[:, 0], cc_all[:, 0],
            hg_all[:, 1], ret_all[:, 1], cc_all[:, 1])
```

```python
import functools
import math

import numpy as np
import jax
import jax.numpy as jnp
from jax import lax
from jax.experimental import pallas as pl
from jax.experimental.pallas import tpu as pltpu

CHUNK = 64
SUB = 8
LEVELS = (64, 32, 16)
N_SLOT = sum(CHUNK // m for m in LEVELS)
N_HEADS = 4
D_HEAD = 128
D_BRANCH = N_HEADS * D_HEAD
PAST_LEN = 2048
ROPE_BASE = 10000.0
NORM_EPS = 1e-6
GN_EPS = 1e-5
F_FLOOR = 1e-30
MID = CHUNK // 2
MID_SPREAD_MAX = 80.0
FFN_BLOCK = 256
CONV_PAD = 8
PIECE = 256
VMEM_LIMIT_BYTES = 60 * 1024 * 1024

_F32 = jnp.float32
_BF16 = jnp.bfloat16


def _dot(a, b):
    return jnp.dot(a, b, preferred_element_type=_F32)


def _dot_nt(a, b):
    return lax.dot_general(a, b, (((1,), (1,)), ((), ())), preferred_element_type=_F32)


def _dot_tn(a, b):
    return lax.dot_general(a, b, (((0,), (0,)), ((), ())), preferred_element_type=_F32)


def _sigmoid(x):
    return jax.nn.sigmoid(x)


def _silu(x):
    return x * jax.nn.sigmoid(x)


def _run(thunks):
    for thunk in thunks:
        thunk()


def _rms_rows(x, w):
    ms = jnp.mean(x * x, axis=-1, keepdims=True)
    return x * lax.rsqrt(ms + NORM_EPS) * w


def _factored_scores(qh, kh, bh, cat1, cat2, p_buf, e8_ref):
    rows = qh.shape[0]
    jrow = lax.broadcasted_iota(jnp.int32, (rows, D_HEAD), 0)
    slot = 0
    for m in LEVELS:
        half = m // 2
        bm = bh.reshape(rows // m, m, D_HEAD)
        delta = bm - bm[:, half - 1:half, :]
        is_q = lax.broadcasted_iota(jnp.int32, bm.shape, 1) >= half
        dec = jnp.exp(jnp.minimum(jnp.where(is_q, delta, -delta), 0.0))
        xm = (jnp.where(is_q, qh.reshape(bm.shape), kh.reshape(bm.shape)) * dec).reshape(rows, D_HEAD)
        hidx = jrow >> int(math.log2(half))
        for s in range(rows // m):
            cs = slice(slot * D_HEAD, (slot + 1) * D_HEAD)
            cat1[:, cs] = jnp.where(hidx == 2 * s + 1, xm, 0.0).astype(_BF16)
            cat2[:, cs] = jnp.where(hidx == 2 * s, xm, 0.0).astype(_BF16)
            slot += 1
    rowg = lax.broadcasted_iota(jnp.int32, (rows // SUB, SUB, D_HEAD), 1)
    qg = qh.reshape(rowg.shape)
    kg = kh.reshape(rowg.shape)
    bg = bh.reshape(rowg.shape)
    for s in range(SUB):
        msk = rowg >= s
        dec = jnp.exp(jnp.where(msk, bg - bg[:, s:s + 1, :], 0.0))
        ps = jnp.where(msk, dec * qg * kg[:, s:s + 1, :], 0.0)
        p_buf[:, s * D_HEAD:(s + 1) * D_HEAD] = ps.reshape(rows, D_HEAD).astype(_BF16)
    ad_col = lax.broadcasted_iota(jnp.int32, (rows, D_HEAD), 1)
    in_block = (ad_col // SUB) == (jrow // SUB)
    ad = jnp.where(in_block, _dot(p_buf[...], e8_ref[...]), 0.0)
    return _dot_nt(cat1[...], cat2[...]) + ad[:, 0:CHUNK]


def _mixer_kernel(nt, first, l_ref, *refs):
    if first:
        xp_ref, xs_ref, *refs = refs
    else:
        x_ref, *refs = refs
    (shg_in_ref, sret_in_ref, nw_ref, win_ref, lbl_ref, hgw_ref, rnw_ref, rnb_ref, wbr_ref, wout_ref, inv_ref,
     tril_ref, e8_ref,
     y_ref, shg_ref, sret_ref,
     h_s, q_s, k_s, bc_s, v_s, rq_s, rk_s, rv_s, sga_s, sgr_s, oa_s, ob_s, oag_s, za_s, m_s, gb_s,
     cat1_s, cat2_s, p_s, am_s) = refs
    t = pl.program_id(0)
    layer = l_ref[0]
    nb, _, dm = y_ref.shape
    rows = nb * CHUNK
    seq_rows = [slice(b * CHUNK, (b + 1) * CHUNK) for b in range(nb)]

    def load_x():
        if first:
            return jnp.where(t == nt, xs_ref[...], xp_ref[...]).reshape(rows, dm)
        return x_ref[...].reshape(rows, dm)

    @pl.when(t == 0)
    def _():
        shg_ref[...] = jnp.zeros(shg_ref.shape, _F32)
        sret_ref[...] = jnp.zeros(sret_ref.shape, _F32)

    @pl.when(t == nt)
    def _():
        shg_ref[0] = shg_in_ref[...]
        sret_ref[0] = sret_in_ref[...]

    h_s[...] = _rms_rows(load_x(), nw_ref[...]).astype(_BF16)

    def proj(group, width=D_BRANCH, offset=0):
        c0 = group * D_BRANCH + offset
        return _dot(h_s[...], win_ref[:, c0:c0 + width])

    logits = lbl_ref[...]
    ex = jnp.exp(logits - jnp.max(logits, axis=0, keepdims=True))
    prob = ex / jnp.sum(ex, axis=0, keepdims=True)
    lrow = lax.broadcasted_iota(jnp.int32, prob.shape, 0)
    lb = jnp.sum(jnp.where((lrow >= 1) & (lrow <= layer), prob, 0.0), axis=0, keepdims=True)

    f_gate = lb + (1.0 - lb) * _sigmoid(proj(1))
    q_s[...] = _silu(proj(0))
    for c in range(0, D_BRANCH, PIECE):
        rq_s[:, c:c + PIECE] = proj(4, PIECE, c)
    k_s[...] = 1.0 - f_gate
    log_f = jnp.log(jnp.maximum(f_gate, F_FLOOR))
    lf_hi = log_f.astype(_BF16)
    r1 = log_f - lf_hi.astype(_F32)
    lf_mid = r1.astype(_BF16)
    lf_lo = (r1 - lf_mid.astype(_F32)).astype(_BF16)
    tril3 = tril_ref[...]
    for rs in seq_rows:
        bc_s[rs, :] = _dot(tril3, jnp.concatenate([lf_hi[rs], lf_mid[rs], lf_lo[rs]], axis=0))
    v_s[...] = proj(2).astype(_BF16)

    jrow = lax.broadcasted_iota(jnp.int32, (rows, D_HEAD), 0) & (CHUNK - 1)
    hgw = hgw_ref[...]

    ball = bc_s[...].reshape(nb, CHUNK, D_BRANCH)
    spread = jnp.max(jnp.abs(ball - ball[:, MID - 1:MID, :]))
    mid_ok = spread <= MID_SPREAD_MAX

    @pl.when(t == 0)
    def _():
        am_s[...] = jnp.zeros(am_s.shape, _F32)

    @pl.when(jnp.logical_not(mid_ok))
    def _():
        def seq_body(b, carry):
            rs = pl.ds(pl.multiple_of(b * CHUNK, CHUNK), CHUNK)
            for hd in range(N_HEADS):
                sl = slice(hd * D_HEAD, (hd + 1) * D_HEAD)
                am_s[hd, rs, :] = _factored_scores(q_s[rs, sl], k_s[rs, sl], bc_s[rs, sl], cat1_s.at[hd],
                                                   cat2_s.at[hd], p_s.at[hd], e8_ref)
            return carry

        lax.fori_loop(0, nb, seq_body, 0)

    causal = (lax.broadcasted_iota(jnp.int32, (CHUNK, CHUNK), 0)
              >= lax.broadcasted_iota(jnp.int32, (CHUNK, CHUNK), 1))

    def proj_piece(dst, group, post, c):
        dst[:, c:c + PIECE] = post(proj(group, PIECE, c))

    def to_bf16(v):
        return v.astype(_BF16)

    def identity(v):
        return v

    def gate_b_piece(c):
        gb_s[:, c:c + PIECE] = _sigmoid(proj(8, PIECE, dm + c))

    hg_units = ((rk_s, 5, identity), (rv_s, 6, to_bf16), (sga_s, 3, _silu), (sgr_s, 7, _silu))
    hg_list = ([functools.partial(proj_piece, dst, group, post, c) for dst, group, post in hg_units
                for c in range(0, D_BRANCH, PIECE)]
               + [functools.partial(gate_b_piece, c) for c in range(0, dm, PIECE)])
    per_hg = -(-len(hg_list) // N_HEADS)
    hg_pieces = [hg_list[i * per_hg:(i + 1) * per_hg] for i in range(N_HEADS)]

    def hg_factors(hd):
        sl = slice(hd * D_HEAD, (hd + 1) * D_HEAD)
        qh = q_s[:, sl]
        bh = bc_s[:, sl]
        b3 = bh.reshape(nb, CHUNK, D_HEAD)
        q3 = qh.reshape(b3.shape)
        k3 = k_s[:, sl].reshape(b3.shape)
        blast = b3[:, CHUNK - 1:CHUNK, :]
        bmid = b3[:, MID - 1:MID, :]
        qe = (qh * jnp.exp(bh)).astype(_BF16)
        kd = (k3 * jnp.exp(blast - b3)).reshape(rows, D_HEAD).astype(_BF16)
        qm = (q3 * jnp.exp(b3 - bmid)).reshape(rows, D_HEAD).astype(_BF16)
        km = (k3 * jnp.exp(bmid - b3)).reshape(rows, D_HEAD).astype(_BF16)
        return qe, kd, qm, km, jnp.exp(blast)

    def hg_state_matmuls(hd, factors, pieces):
        qe, kd, qm, km, chunk_dec = factors
        sl = slice(hd * D_HEAD, (hd + 1) * D_HEAD)
        vh = v_s[:, sl]
        states = [shg_ref[0, b, hd] for b in range(nb)]
        amats = [jnp.where(mid_ok, jnp.where(causal, _dot_nt(qm[rs], km[rs]), 0.0), am_s[hd, rs, :])
                 for rs in seq_rows]
        _run(pieces[0])
        outs = [_dot(amats[b].astype(_BF16), vh[rs]) + _dot_nt(qe[rs], states[b].astype(_BF16))
                for b, rs in enumerate(seq_rows)]
        _run(pieces[1])
        for b, rs in enumerate(seq_rows):
            shg_ref[0, b, hd] = states[b] * chunk_dec[b] + _dot_tn(vh[rs], kd[rs])
        oa_s[:, sl] = _rms_rows(jnp.concatenate(outs, axis=0), hgw)

    factors = hg_factors(0)
    for hd in range(N_HEADS):
        mine = hg_pieces[hd]
        _run(mine[0:1])
        upcoming = hg_factors(hd + 1) if hd + 1 < N_HEADS else None
        hg_state_matmuls(hd, factors, (mine[1:2], mine[2:]))
        factors = upcoming

    pos0 = jnp.where(t == nt, PAST_LEN, t * CHUNK)
    posf = (lax.broadcasted_iota(jnp.int32, (CHUNK, D_HEAD), 0) + pos0).astype(_F32)
    ang = posf * inv_ref[...]
    lane = lax.broadcasted_iota(jnp.int32, (CHUNK, D_HEAD), 1)
    cosf = jnp.cos(ang)
    sinf = jnp.where(lane < D_HEAD // 2, -1.0, 1.0) * jnp.sin(ang)
    cos3 = jnp.broadcast_to(cosf[None], (nb, CHUNK, D_HEAD)).reshape(rows, D_HEAD)
    sin3 = jnp.broadcast_to(sinf[None], (nb, CHUNK, D_HEAD)).reshape(rows, D_HEAD)

    def rope(v):
        return v * cos3 + pltpu.roll(v, D_HEAD // 2, 1) * sin3

    def za_piece(c):
        za_s[:, c:c + PIECE] = _dot(oag_s[...], wbr_ref[0, :, c:c + PIECE])

    def gate_a_piece(c):
        m_s[:, c:c + PIECE] = _sigmoid(proj(8, PIECE, c)) * za_s[:, c:c + PIECE]

    oag_s[...] = (oa_s[...] * sga_s[...]).astype(_BF16)
    ret_list = ([functools.partial(za_piece, c) for c in range(0, dm, PIECE)]
                + [functools.partial(gate_a_piece, c) for c in range(0, dm, PIECE)])
    per_head = -(-len(ret_list) // N_HEADS)
    ret_pieces = [ret_list[i * per_head:(i + 1) * per_head] for i in range(N_HEADS)]

    idx_r = jrow.astype(_F32)
    rel = (lax.broadcasted_iota(jnp.int32, (CHUNK, CHUNK), 0)
           - lax.broadcasted_iota(jnp.int32, (CHUNK, CHUNK), 1)).astype(_F32)
    rnw = rnw_ref[...]
    rnb = rnb_ref[...]
    log_gamma = [math.log(1.0 - 2.0 ** (-5.0 - hd)) for hd in range(N_HEADS)]

    def ret_factors(hd):
        sl = slice(hd * D_HEAD, (hd + 1) * D_HEAD)
        lg = log_gamma[hd]
        kf = rope(rk_s[:, sl]) * (D_HEAD ** -0.5)
        qf = rope(rq_s[:, sl])
        kdec = (kf * jnp.exp(lg * (CHUNK - 1.0 - idx_r))).astype(_BF16)
        qdec = (qf * jnp.exp(lg * (idx_r + 1.0))).astype(_BF16)
        dmat = jnp.where(rel >= 0, jnp.exp(lg * jnp.maximum(rel, 0.0)), 0.0)
        return qf.astype(_BF16), kf.astype(_BF16), kdec, qdec, dmat

    def ret_state_matmuls(hd, factors, pieces):
        qb, kb, kdec, qdec, dmat = factors
        sl = slice(hd * D_HEAD, (hd + 1) * D_HEAD)
        vh = rv_s[:, sl]
        states = [sret_ref[0, b, hd] for b in range(nb)]
        amats = [_dot_nt(qb[rs], kb[rs]) * dmat for rs in seq_rows]
        _run(pieces[0])
        outs = [_dot(amats[b].astype(_BF16), vh[rs]) + _dot(qdec[rs], states[b].astype(_BF16))
                for b, rs in enumerate(seq_rows)]
        _run(pieces[1])
        for b, rs in enumerate(seq_rows):
            sret_ref[0, b, hd] = math.exp(log_gamma[hd] * CHUNK) * states[b] + _dot_tn(kdec[rs], vh[rs])
        o = jnp.concatenate(outs, axis=0)
        mu = jnp.mean(o, axis=-1, keepdims=True)
        oc = o - mu
        var = jnp.mean(oc * oc, axis=-1, keepdims=True)
        ob_s[:, sl] = oc * lax.rsqrt(var + GN_EPS) * rnw[:, sl] + rnb[:, sl]

    factors = ret_factors(0)
    for hd in range(N_HEADS):
        mine = ret_pieces[hd]
        _run(mine[0:1])
        upcoming = ret_factors(hd + 1) if hd + 1 < N_HEADS else None
        ret_state_matmuls(hd, factors, (mine[1:2], mine[2:]))
        factors = upcoming

    z_b = _dot((ob_s[...] * sgr_s[...]).astype(_BF16), wbr_ref[1])
    merged = m_s[...] + gb_s[...] * z_b
    y = load_x() + _dot(merged.astype(_BF16), wout_ref[...])
    y_ref[...] = y.reshape(y_ref.shape)


def _layer_spec(block):
    zeros = (0,) * (len(block) - 1)
    return pl.BlockSpec(block, lambda t, l: (l[0],) + zeros, pipeline_mode=pl.Buffered(1))


def _const_spec(block):
    zeros = (0,) * len(block)
    return pl.BlockSpec(block, lambda t, l: zeros, pipeline_mode=pl.Buffered(1))


def _mixer_call(nt, lv, xs_in, shg_in, sret_in, norm_w, w_in, lb_logits, hg_norm_w, ret_norm_w,
                ret_norm_b, w_branch, w_out, inv2, tril3, e8):
    first = len(xs_in) == 2
    nb, _, dm = xs_in[0].shape
    depth = w_in.shape[0]
    rows = nb * CHUNK
    d_in = w_in.shape[2]
    st_block = (None, nb, N_HEADS, D_HEAD, D_HEAD)
    x_spec = pl.BlockSpec((nb, CHUNK, dm), lambda t, l: (0, t, 0))
    if first:
        x_specs = [pl.BlockSpec((nb, CHUNK, dm), lambda t, l: (0, jnp.minimum(t, nt - 1), 0)),
                   _const_spec((nb, CHUNK, dm))]
    else:
        x_specs = [x_spec]
    in_specs = x_specs + [
        _layer_spec(st_block),
        _layer_spec(st_block),
        _layer_spec((None, 1, dm)),
        _layer_spec((None, dm, d_in)),
        _const_spec((depth, D_BRANCH)),
        _layer_spec((None, 1, D_HEAD)),
        _layer_spec((None, 1, D_BRANCH)),
        _layer_spec((None, 1, D_BRANCH)),
        _layer_spec((None, 2, D_BRANCH, dm)),
        _layer_spec((None, dm, dm)),
        _const_spec((1, D_HEAD)),
        _const_spec((CHUNK, 3 * CHUNK)),
        _const_spec((SUB * D_HEAD, D_HEAD)),
    ]
    st_out = jax.ShapeDtypeStruct((2, nb, N_HEADS, D_HEAD, D_HEAD), _F32)
    st_out_spec = pl.BlockSpec((1, nb, N_HEADS, D_HEAD, D_HEAD), lambda t, l: (t // nt, 0, 0, 0, 0))
    scratch = [
        pltpu.VMEM((rows, dm), _BF16),
        pltpu.VMEM((rows, D_BRANCH), _F32),
        pltpu.VMEM((rows, D_BRANCH), _F32),
        pltpu.VMEM((rows, D_BRANCH), _F32),
        pltpu.VMEM((rows, D_BRANCH), _BF16),
        pltpu.VMEM((rows, D_BRANCH), _F32),
        pltpu.VMEM((rows, D_BRANCH), _F32),
        pltpu.VMEM((rows, D_BRANCH), _BF16),
        pltpu.VMEM((rows, D_BRANCH), _F32),
        pltpu.VMEM((rows, D_BRANCH), _F32),
        pltpu.VMEM((rows, D_BRANCH), _F32),
        pltpu.VMEM((rows, D_BRANCH), _F32),
        pltpu.VMEM((rows, D_BRANCH), _BF16),
        pltpu.VMEM((rows, dm), _F32),
        pltpu.VMEM((rows, dm), _F32),
        pltpu.VMEM((rows, dm), _F32),
        pltpu.VMEM((N_HEADS, CHUNK, N_SLOT * D_HEAD), _BF16),
        pltpu.VMEM((N_HEADS, CHUNK, N_SLOT * D_HEAD), _BF16),
        pltpu.VMEM((N_HEADS, CHUNK, SUB * D_HEAD), _BF16),
        pltpu.VMEM((N_HEADS, rows, CHUNK), _F32),
    ]
    grid_spec = pltpu.PrefetchScalarGridSpec(
        num_scalar_prefetch=1, grid=(nt + 1,), in_specs=in_specs,
        out_specs=[x_spec, st_out_spec, st_out_spec], scratch_shapes=scratch)
    return pl.pallas_call(
        functools.partial(_mixer_kernel, nt, first),
        grid_spec=grid_spec,
        out_shape=[jax.ShapeDtypeStruct((nb, (nt + 1) * CHUNK, dm), _F32), st_out, st_out],
        input_output_aliases={} if first else {1: 0},
        compiler_params=pltpu.CompilerParams(dimension_semantics=("arbitrary",),
                                             vmem_limit_bytes=VMEM_LIMIT_BYTES),
        name="mixer_first" if first else "mixer",
    )(lv, *xs_in, shg_in, sret_in, norm_w, w_in, lb_logits, hg_norm_w, ret_norm_w, ret_norm_b, w_branch,
      w_out, inv2, tril3, e8)


def _ffn_kernel(nt, last, l_ref, x_ref, cc_in_ref, nw_ref, wup_ref, cw_ref, cb_ref, wd_ref, nfw_ref, *refs):
    if last:
        yp_ref, ys_ref, cc_ref, h_s, acc_s, ext_s, g_s = refs
    else:
        y_ref, cc_ref, h_s, acc_s, ext_s, g_s = refs
    t = pl.program_id(0)
    nb, _, dm = x_ref.shape
    rows = nb * CHUNK
    d_ff = wd_ref.shape[0]
    fb = FFN_BLOCK
    nblk = d_ff // fb

    @pl.when(t == 0)
    def _():
        cc_ref[...] = jnp.zeros(cc_ref.shape, _F32)

    @pl.when(t == nt)
    def _():
        cc_ref[0] = cc_in_ref[...]

    h_s[...] = _rms_rows(x_ref[...].reshape(rows, dm), nw_ref[...]).astype(_BF16)

    def up(i):
        h = h_s[...]
        cols = slice(i * fb, (i + 1) * fb)
        gcols = slice(d_ff + i * fb, d_ff + (i + 1) * fb)
        ext_s[i % 2, :, CONV_PAD:CONV_PAD + CHUNK, :] = _dot(h, wup_ref[:, cols]).reshape(nb, CHUNK, fb)
        g_s[i % 2] = _dot(h, wup_ref[:, gcols])

    def down(i):
        cols = slice(i * fb, (i + 1) * fb)
        ext = ext_s.at[i % 2]
        ext[:, CONV_PAD - 2:CONV_PAD, :] = cc_ref[0, :, :, cols]
        a3 = ext[:, CONV_PAD:CONV_PAD + CHUNK, :]
        cw = cw_ref[:, cols]
        conv = (cb_ref[:, cols] + ext[:, CONV_PAD - 2:CONV_PAD - 2 + CHUNK, :] * cw[0:1, :]
                + ext[:, CONV_PAD - 1:CONV_PAD - 1 + CHUNK, :] * cw[1:2, :] + a3 * cw[2:3, :])
        cc_ref[0, :, :, cols] = a3[:, CHUNK - 2:CHUNK, :]
        u = (_silu(conv).reshape(rows, fb) * g_s[i % 2]).astype(_BF16)
        return _dot(u, wd_ref[cols, :])

    up(0)
    y = None
    for i in range(nblk):
        if i + 1 < nblk:
            up(i + 1)
        part = down(i)
        prev = x_ref[...].reshape(rows, dm) if i == 0 else acc_s[...]
        if i + 1 < nblk:
            acc_s[...] = prev + part
        else:
            y = prev + part

    if last:
        yn = _rms_rows(y, nfw_ref[...]).reshape(nb, CHUNK, dm)

        @pl.when(t < nt)
        def _():
            yp_ref[...] = yn

        @pl.when(t == nt)
        def _():
            ys_ref[...] = yn
    else:
        y_ref[...] = y.reshape(y_ref.shape)


def _ffn_call(nt, last, lv, x, cc_in, norm_w, w_up, cw, cb, wd, nfw):
    nb, _, dm = x.shape
    rows = nb * CHUNK
    d_ff = wd.shape[1]
    x_spec = pl.BlockSpec((nb, CHUNK, dm), lambda t, l: (0, t, 0))
    in_specs = [
        x_spec,
        _layer_spec((None, nb, 2, d_ff)),
        _layer_spec((None, 1, dm)),
        _layer_spec((None, dm, 2 * d_ff)),
        _layer_spec((None, 3, d_ff)),
        _layer_spec((None, 1, d_ff)),
        _layer_spec((None, d_ff, dm)),
        _const_spec((1, dm)),
    ]
    cc_out = jax.ShapeDtypeStruct((2, nb, 2, d_ff), _F32)
    cc_spec = pl.BlockSpec((1, nb, 2, d_ff), lambda t, l: (t // nt, 0, 0, 0))
    if last:
        out_shape = [jax.ShapeDtypeStruct((nb, nt * CHUNK, dm), _F32), jax.ShapeDtypeStruct((nb, CHUNK, dm), _F32), cc_out]
        out_specs = [pl.BlockSpec((nb, CHUNK, dm), lambda t, l: (0, jnp.minimum(t, nt - 1), 0)),
                     pl.BlockSpec((nb, CHUNK, dm), lambda t, l: (0, 0, 0)), cc_spec]
    else:
        out_shape = [jax.ShapeDtypeStruct(x.shape, _F32), cc_out]
        out_specs = [x_spec, cc_spec]
    scratch = [
        pltpu.VMEM((rows, dm), _BF16),
        pltpu.VMEM((rows, dm), _F32),
        pltpu.VMEM((2, nb, CHUNK + CONV_PAD, FFN_BLOCK), _F32),
        pltpu.VMEM((2, rows, FFN_BLOCK), _F32),
    ]
    grid_spec = pltpu.PrefetchScalarGridSpec(
        num_scalar_prefetch=1, grid=(nt + 1,), in_specs=in_specs, out_specs=out_specs, scratch_shapes=scratch)
    return pl.pallas_call(
        functools.partial(_ffn_kernel, nt, last),
        grid_spec=grid_spec,
        out_shape=out_shape,
        input_output_aliases={} if last else {1: 0},
        compiler_params=pltpu.CompilerParams(dimension_semantics=("arbitrary",),
                                             vmem_limit_bytes=VMEM_LIMIT_BYTES),
        name="ffn_last" if last else "ffn",
    )(lv, x, cc_in, norm_w, w_up, cw, cb, wd, nfw)


def _constants():
    half = D_HEAD // 2
    inv = ROPE_BASE ** (-jnp.arange(half, dtype=_F32) / half)
    inv2 = jnp.concatenate([inv, inv])[None, :]
    r = np.arange(CHUNK)
    tril = (r[:, None] >= r[None, :]).astype(np.float32)
    tril3 = jnp.asarray(np.concatenate([tril, tril, tril], axis=1), _BF16)
    e8 = np.zeros((SUB * D_HEAD, D_HEAD), np.float32)
    cols = np.arange(CHUNK)
    for s in range(SUB):
        e8[s * D_HEAD:(s + 1) * D_HEAD, cols[cols % SUB == s]] = 1.0
    return inv2, tril3, jnp.asarray(e8, _BF16)


def kernel(x_prompt, x_sample, state_hgrn, state_ret, cache_conv, norm_mix_w, w_in, hg_lb_logits,
           hg_norm_w, ret_norm_w, ret_norm_b, w_branch, w_out, norm_ffn_w, w_up, conv_w, conv_b,
           w_down, norm_final_w):
    nb, tp, dm = x_prompt.shape
    depth = w_in.shape[0]
    d_ff = w_down.shape[1]
    assert x_sample.shape == (nb, CHUNK, dm) and tp % CHUNK == 0 and d_ff % FFN_BLOCK == 0
    nt = tp // CHUNK
    inv2, tril3, e8 = _constants()

    w_in_b = w_in.astype(_BF16)
    w_branch_b = w_branch.astype(_BF16)
    w_out_b = w_out.astype(_BF16)
    w_up_b = w_up.astype(_BF16)
    w_down_b = w_down.astype(_BF16)
    shg_in = jnp.swapaxes(state_hgrn, -1, -2)
    nmw = norm_mix_w[:, None, :]
    nfw_l = norm_ffn_w[:, None, :]
    hgw = hg_norm_w[:, None, :]
    rnw = ret_norm_w[:, None, :]
    rnb = ret_norm_b[:, None, :]
    cb = conv_b[:, None, :]
    nfin = norm_final_w[None, :]

    xs = (x_prompt, x_sample)
    hg_all, ret_all, cc_all = [], [], []
    for l in range(depth):
        lv = jnp.full((1,), l, jnp.int32)
        x, s_hg, s_ret = _mixer_call(nt, lv, xs, shg_in, state_ret, nmw, w_in_b, hg_lb_logits, hgw, rnw, rnb,
                                     w_branch_b, w_out_b, inv2, tril3, e8)
        *xs, s_cc = _ffn_call(nt, l == depth - 1, lv, x, cache_conv, nfw_l, w_up_b, conv_w, cb, w_down_b, nfin)
        hg_all.append(s_hg)
        ret_all.append(s_ret)
        cc_all.append(s_cc)
    y_prompt, y_sample = xs
    hg_all = jnp.swapaxes(jnp.stack(hg_all), -1, -2)
    ret_all = jnp.stack(ret_all)
    cc_all = jnp.stack(cc_all)
    return (y_prompt, y_sample, hg_all[:, 0], ret_all[:, 0], cc_all[:, 0],
            hg_all[:, 1], ret_all[:, 1], cc_all[:, 1])
```

```python
import functools
import math

import numpy as np
import jax
import jax.numpy as jnp
from jax import lax
from jax.experimental import pallas as pl
from jax.experimental.pallas import tpu as pltpu

CHUNK = 64
SUB = 8
LEVELS = (64, 32, 16)
N_SLOT = sum(CHUNK // m for m in LEVELS)
N_HEADS = 4
D_HEAD = 128
D_BRANCH = N_HEADS * D_HEAD
PAST_LEN = 2048
ROPE_BASE = 10000.0
NORM_EPS = 1e-6
GN_EPS = 1e-5
F_FLOOR = 1e-30
MID = CHUNK // 2
MID_SPREAD_MAX = 80.0
FFN_BLOCK = 256
CONV_PAD = 8
PIECE = 256
VMEM_LIMIT_BYTES = 60 * 1024 * 1024

_F32 = jnp.float32
_BF16 = jnp.bfloat16


def _dot(a, b):
    return jnp.dot(a, b, preferred_element_type=_F32)


def _dot_nt(a, b):
    return lax.dot_general(a, b, (((1,), (1,)), ((), ())), preferred_element_type=_F32)


def _dot_tn(a, b):
    return lax.dot_general(a, b, (((0,), (0,)), ((), ())), preferred_element_type=_F32)


def _sigmoid(x):
    return jax.nn.sigmoid(x)


def _silu(x):
    return x * jax.nn.sigmoid(x)


def _run(thunks):
    for thunk in thunks:
        thunk()


def _rms_rows(x, w):
    ms = jnp.mean(x * x, axis=-1, keepdims=True)
    return x * lax.rsqrt(ms + NORM_EPS) * w


def _factored_scores(qh, kh, bh, cat1, cat2, p_buf, e8_ref):
    rows = qh.shape[0]
    jrow = lax.broadcasted_iota(jnp.int32, (rows, D_HEAD), 0)
    slot = 0
    for m in LEVELS:
        half = m // 2
        bm = bh.reshape(rows // m, m, D_HEAD)
        delta = bm - bm[:, half - 1:half, :]
        is_q = lax.broadcasted_iota(jnp.int32, bm.shape, 1) >= half
        dec = jnp.exp(jnp.minimum(jnp.where(is_q, delta, -delta), 0.0))
        xm = (jnp.where(is_q, qh.reshape(bm.shape), kh.reshape(bm.shape)) * dec).reshape(rows, D_HEAD)
        hidx = jrow >> int(math.log2(half))
        for s in range(rows // m):
            cs = slice(slot * D_HEAD, (slot + 1) * D_HEAD)
            cat1[:, cs] = jnp.where(hidx == 2 * s + 1, xm, 0.0).astype(_BF16)
            cat2[:, cs] = jnp.where(hidx == 2 * s, xm, 0.0).astype(_BF16)
            slot += 1
    rowg = lax.broadcasted_iota(jnp.int32, (rows // SUB, SUB, D_HEAD), 1)
    qg = qh.reshape(rowg.shape)
    kg = kh.reshape(rowg.shape)
    bg = bh.reshape(rowg.shape)
    for s in range(SUB):
        msk = rowg >= s
        dec = jnp.exp(jnp.where(msk, bg - bg[:, s:s + 1, :], 0.0))
        ps = jnp.where(msk, dec * qg * kg[:, s:s + 1, :], 0.0)
        p_buf[:, s * D_HEAD:(s + 1) * D_HEAD] = ps.reshape(rows, D_HEAD).astype(_BF16)
    ad_col = lax.broadcasted_iota(jnp.int32, (rows, D_HEAD), 1)
    in_block = (ad_col // SUB) == (jrow // SUB)
    ad = jnp.where(in_block, _dot(p_buf[...], e8_ref[...]), 0.0)
    return _dot_nt(cat1[...], cat2[...]) + ad[:, 0:CHUNK]


def _mixer_kernel(nt, first, l_ref, *refs):
    if first:
        xp_ref, xs_ref, *refs = refs
    else:
        x_ref, *refs = refs
    (shg_in_ref, sret_in_ref, nw_ref, win_ref, lbl_ref, hgw_ref, rnw_ref, rnb_ref, wbr_ref, wout_ref, inv_ref,
     tril_ref, e8_ref,
     y_ref, shg_ref, sret_ref,
     h_s, q_s, k_s, bc_s, v_s, rq_s, rk_s, rv_s, sga_s, sgr_s, oa_s, ob_s, oag_s, za_s, m_s, gb_s,
     cat1_s, cat2_s, p_s, am_s) = refs
    t = pl.program_id(0)
    layer = l_ref[0]
    nb, _, dm = y_ref.shape
    rows = nb * CHUNK
    seq_rows = [slice(b * CHUNK, (b + 1) * CHUNK) for b in range(nb)]

    def load_x():
        if first:
            return jnp.where(t == nt, xs_ref[...], xp_ref[...]).reshape(rows, dm)
        return x_ref[...].reshape(rows, dm)

    @pl.when(t == 0)
    def _():
        shg_ref[...] = jnp.zeros(shg_ref.shape, _F32)
        sret_ref[...] = jnp.zeros(sret_ref.shape, _F32)

    @pl.when(t == nt)
    def _():
        shg_ref[0] = shg_in_ref[...]
        sret_ref[0] = sret_in_ref[...]

    h_s[...] = _rms_rows(load_x(), nw_ref[...]).astype(_BF16)

    def proj(group, width=D_BRANCH, offset=0):
        c0 = group * D_BRANCH + offset
        return _dot(h_s[...], win_ref[:, c0:c0 + width])

    logits = lbl_ref[...]
    ex = jnp.exp(logits - jnp.max(logits, axis=0, keepdims=True))
    prob = ex / jnp.sum(ex, axis=0, keepdims=True)
    lrow = lax.broadcasted_iota(jnp.int32, prob.shape, 0)
    lb = jnp.sum(jnp.where((lrow >= 1) & (lrow <= layer), prob, 0.0), axis=0, keepdims=True)

    f_gate = lb + (1.0 - lb) * _sigmoid(proj(1))
    q_s[...] = _silu(proj(0))
    for c in range(0, D_BRANCH, PIECE):
        rq_s[:, c:c + PIECE] = proj(4, PIECE, c)
    k_s[...] = 1.0 - f_gate
    log_f = jnp.log(jnp.maximum(f_gate, F_FLOOR))
    lf_hi = log_f.astype(_BF16)
    r1 = log_f - lf_hi.astype(_F32)
    lf_mid = r1.astype(_BF16)
    lf_lo = (r1 - lf_mid.astype(_F32)).astype(_BF16)
    tril3 = tril_ref[...]
    for rs in seq_rows:
        bc_s[rs, :] = _dot(tril3, jnp.concatenate([lf_hi[rs], lf_mid[rs], lf_lo[rs]], axis=0))
    v_s[...] = proj(2).astype(_BF16)

    jrow = lax.broadcasted_iota(jnp.int32, (rows, D_HEAD), 0) & (CHUNK - 1)
    hgw = hgw_ref[...]

    ball = bc_s[...].reshape(nb, CHUNK, D_BRANCH)
    spread = jnp.max(jnp.abs(ball - ball[:, MID - 1:MID, :]))
    mid_ok = spread <= MID_SPREAD_MAX

    @pl.when(t == 0)
    def _():
        am_s[...] = jnp.zeros(am_s.shape, _F32)

    @pl.when(jnp.logical_not(mid_ok))
    def _():
        def seq_body(b, carry):
            rs = pl.ds(pl.multiple_of(b * CHUNK, CHUNK), CHUNK)
            for hd in range(N_HEADS):
                sl = slice(hd * D_HEAD, (hd + 1) * D_HEAD)
                am_s[hd, rs, :] = _factored_scores(q_s[rs, sl], k_s[rs, sl], bc_s[rs, sl], cat1_s.at[hd],
                                                   cat2_s.at[hd], p_s.at[hd], e8_ref)
            return carry

        lax.fori_loop(0, nb, seq_body, 0)

    causal = (lax.broadcasted_iota(jnp.int32, (CHUNK, CHUNK), 0)
              >= lax.broadcasted_iota(jnp.int32, (CHUNK, CHUNK), 1))

    def proj_piece(dst, group, post, c):
        dst[:, c:c + PIECE] = post(proj(group, PIECE, c))

    def to_bf16(v):
        return v.astype(_BF16)

    def identity(v):
        return v

    def gate_b_piece(c):
        gb_s[:, c:c + PIECE] = _sigmoid(proj(8, PIECE, dm + c))

    hg_units = ((rk_s, 5, identity), (rv_s, 6, to_bf16), (sga_s, 3, _silu), (sgr_s, 7, _silu))
    hg_list = ([functools.partial(proj_piece, dst, group, post, c) for dst, group, post in hg_units
                for c in range(0, D_BRANCH, PIECE)]
               + [functools.partial(gate_b_piece, c) for c in range(0, dm, PIECE)])
    per_hg = -(-len(hg_list) // N_HEADS)
    hg_pieces = [hg_list[i * per_hg:(i + 1) * per_hg] for i in range(N_HEADS)]

    def hg_factors(hd):
        sl = slice(hd * D_HEAD, (hd + 1) * D_HEAD)
        qh = q_s[:, sl]
        bh = bc_s[:, sl]
        b3 = bh.reshape(nb, CHUNK, D_HEAD)
        q3 = qh.reshape(b3.shape)
        k3 = k_s[:, sl].reshape(b3.shape)
        blast = b3[:, CHUNK - 1:CHUNK, :]
        bmid = b3[:, MID - 1:MID, :]
        qe = (qh * jnp.exp(bh)).astype(_BF16)
        kd = (k3 * jnp.exp(blast - b3)).reshape(rows, D_HEAD).astype(_BF16)
        qm = (q3 * jnp.exp(b3 - bmid)).reshape(rows, D_HEAD).astype(_BF16)
        km = (k3 * jnp.exp(bmid - b3)).reshape(rows, D_HEAD).astype(_BF16)
        return qe, kd, qm, km, jnp.exp(blast)

    def hg_state_matmuls(hd, factors, pieces):
        qe, kd, qm, km, chunk_dec = factors
        sl = slice(hd * D_HEAD, (hd + 1) * D_HEAD)
        vh = v_s[:, sl]
        states = [shg_ref[0, b, hd] for b in range(nb)]
        amats = [jnp.where(mid_ok, jnp.where(causal, _dot_nt(qm[rs], km[rs]), 0.0), am_s[hd, rs, :])
                 for rs in seq_rows]
        _run(pieces[0])
        outs = [_dot(amats[b].astype(_BF16), vh[rs]) + _dot_nt(qe[rs], states[b].astype(_BF16))
                for b, rs in enumerate(seq_rows)]
        _run(pieces[1])
        for b, rs in enumerate(seq_rows):
            shg_ref[0, b, hd] = states[b] * chunk_dec[b] + _dot_tn(vh[rs], kd[rs])
        oa_s[:, sl] = _rms_rows(jnp.concatenate(outs, axis=0), hgw)

    factors = hg_factors(0)
    for hd in range(N_HEADS):
        mine = hg_pieces[hd]
        _run(mine[0:1])
        upcoming = hg_factors(hd + 1) if hd + 1 < N_HEADS else None
        hg_state_matmuls(hd, factors, (mine[1:2], mine[2:]))
        factors = upcoming

    pos0 = jnp.where(t == nt, PAST_LEN, t * CHUNK)
    posf = (lax.broadcasted_iota(jnp.int32, (CHUNK, D_HEAD), 0) + pos0).astype(_F32)
    ang = posf * inv_ref[...]
    lane = lax.broadcasted_iota(jnp.int32, (CHUNK, D_HEAD), 1)
    cosf = jnp.cos(ang)
    sinf = jnp.where(lane < D_HEAD // 2, -1.0, 1.0) * jnp.sin(ang)
    cos3 = jnp.broadcast_to(cosf[None], (nb, CHUNK, D_HEAD)).reshape(rows, D_HEAD)
    sin3 = jnp.broadcast_to(sinf[None], (nb, CHUNK, D_HEAD)).reshape(rows, D_HEAD)

    def rope(v):
        return v * cos3 + pltpu.roll(v, D_HEAD // 2, 1) * sin3

    def za_piece(c):
        za_s[:, c:c + PIECE] = _dot(oag_s[...], wbr_ref[0, :, c:c + PIECE])

    def gate_a_piece(c):
        m_s[:, c:c + PIECE] = _sigmoid(proj(8, PIECE, c)) * za_s[:, c:c + PIECE]

    oag_s[...] = (oa_s[...] * sga_s[...]).astype(_BF16)
    ret_list = ([functools.partial(za_piece, c) for c in range(0, dm, PIECE)]
                + [functools.partial(gate_a_piece, c) for c in range(0, dm, PIECE)])
    per_head = -(-len(ret_list) // N_HEADS)
    ret_pieces = [ret_list[i * per_head:(i + 1) * per_head] for i in range(N_HEADS)]

    idx_r = jrow.astype(_F32)
    rel = (lax.broadcasted_iota(jnp.int32, (CHUNK, CHUNK), 0)
           - lax.broadcasted_iota(jnp.int32, (CHUNK, CHUNK), 1)).astype(_F32)
    rnw = rnw_ref[...]
    rnb = rnb_ref[...]
    log_gamma = [math.log(1.0 - 2.0 ** (-5.0 - hd)) for hd in range(N_HEADS)]

    def ret_factors(hd):
        sl = slice(hd * D_HEAD, (hd + 1) * D_HEAD)
        lg = log_gamma[hd]
        kf = rope(rk_s[:, sl]) * (D_HEAD ** -0.5)
        qf = rope(rq_s[:, sl])
        kdec = (kf * jnp.exp(lg * (CHUNK - 1.0 - idx_r))).astype(_BF16)
        qdec = (qf * jnp.exp(lg * (idx_r + 1.0))).astype(_BF16)
        dmat = jnp.where(rel >= 0, jnp.exp(lg * jnp.maximum(rel, 0.0)), 0.0)
        return qf.astype(_BF16), kf.astype(_BF16), kdec, qdec, dmat

    def ret_state_matmuls(hd, factors, pieces):
        qb, kb, kdec, qdec, dmat = factors
        sl = slice(hd * D_HEAD, (hd + 1) * D_HEAD)
        vh = rv_s[:, sl]
        states = [sret_ref[0, b, hd] for b in range(nb)]
        amats = [_dot_nt(qb[rs], kb[rs]) * dmat for rs in seq_rows]
        _run(pieces[0])
        outs = [_dot(amats[b].astype(_BF16), vh[rs]) + _dot(qdec[rs], states[b].astype(_BF16))
                for b, rs in enumerate(seq_rows)]
        _run(pieces[1])
        for b, rs in enumerate(seq_rows):
            sret_ref[0, b, hd] = math.exp(log_gamma[hd] * CHUNK) * states[b] + _dot_tn(kdec[rs], vh[rs])
        o = jnp.concatenate(outs, axis=0)
        mu = jnp.mean(o, axis=-1, keepdims=True)
        oc = o - mu
        var = jnp.mean(oc * oc, axis=-1, keepdims=True)
        ob_s[:, sl] = oc * lax.rsqrt(var + GN_EPS) * rnw[:, sl] + rnb[:, sl]

    factors = ret_factors(0)
    for hd in range(N_HEADS):
        mine = ret_pieces[hd]
        _run(mine[0:1])
        upcoming = ret_factors(hd + 1) if hd + 1 < N_HEADS else None
        ret_state_matmuls(hd, factors, (mine[1:2], mine[2:]))
        factors = upcoming

    z_b = _dot((ob_s[...] * sgr_s[...]).astype(_BF16), wbr_ref[1])
    merged = m_s[...] + gb_s[...] * z_b
    y = load_x() + _dot(merged.astype(_BF16), wout_ref[...])
    y_ref[...] = y.reshape(y_ref.shape)


def _layer_spec(block):
    zeros = (0,) * (len(block) - 1)
    return pl.BlockSpec(block, lambda t, l: (l[0],) + zeros, pipeline_mode=pl.Buffered(1))


def _const_spec(block):
    zeros = (0,) * len(block)
    return pl.BlockSpec(block, lambda t, l: zeros, pipeline_mode=pl.Buffered(1))


def _mixer_call(nt, lv, xs_in, shg_in, sret_in, norm_w, w_in, lb_logits, hg_norm_w, ret_norm_w,
                ret_norm_b, w_branch, w_out, inv2, tril3, e8):
    first = len(xs_in) == 2
    nb, _, dm = xs_in[0].shape
    depth = w_in.shape[0]
    rows = nb * CHUNK
    d_in = w_in.shape[2]
    st_block = (None, nb, N_HEADS, D_HEAD, D_HEAD)
    x_spec = pl.BlockSpec((nb, CHUNK, dm), lambda t, l: (0, t, 0))
    if first:
        x_specs = [pl.BlockSpec((nb, CHUNK, dm), lambda t, l: (0, jnp.minimum(t, nt - 1), 0)),
                   _const_spec((nb, CHUNK, dm))]
    else:
        x_specs = [x_spec]
    in_specs = x_specs + [
        _layer_spec(st_block),
        _layer_spec(st_block),
        _layer_spec((None, 1, dm)),
        _layer_spec((None, dm, d_in)),
        _const_spec((depth, D_BRANCH)),
        _layer_spec((None, 1, D_HEAD)),
        _layer_spec((None, 1, D_BRANCH)),
        _layer_spec((None, 1, D_BRANCH)),
        _layer_spec((None, 2, D_BRANCH, dm)),
        _layer_spec((None, dm, dm)),
        _const_spec((1, D_HEAD)),
        _const_spec((CHUNK, 3 * CHUNK)),
        _const_spec((SUB * D_HEAD, D_HEAD)),
    ]
    st_out = jax.ShapeDtypeStruct((2, nb, N_HEADS, D_HEAD, D_HEAD), _F32)
    st_out_spec = pl.BlockSpec((1, nb, N_HEADS, D_HEAD, D_HEAD), lambda t, l: (t // nt, 0, 0, 0, 0))
    scratch = [
        pltpu.VMEM((rows, dm), _BF16),
        pltpu.VMEM((rows, D_BRANCH), _F32),
        pltpu.VMEM((rows, D_BRANCH), _F32),
        pltpu.VMEM((rows, D_BRANCH), _F32),
        pltpu.VMEM((rows, D_BRANCH), _BF16),
        pltpu.VMEM((rows, D_BRANCH), _F32),
        pltpu.VMEM((rows, D_BRANCH), _F32),
        pltpu.VMEM((rows, D_BRANCH), _BF16),
        pltpu.VMEM((rows, D_BRANCH), _F32),
        pltpu.VMEM((rows, D_BRANCH), _F32),
        pltpu.VMEM((rows, D_BRANCH), _F32),
        pltpu.VMEM((rows, D_BRANCH), _F32),
        pltpu.VMEM((rows, D_BRANCH), _BF16),
        pltpu.VMEM((rows, dm), _F32),
        pltpu.VMEM((rows, dm), _F32),
        pltpu.VMEM((rows, dm), _F32),
        pltpu.VMEM((N_HEADS, CHUNK, N_SLOT * D_HEAD), _BF16),
        pltpu.VMEM((N_HEADS, CHUNK, N_SLOT * D_HEAD), _BF16),
        pltpu.VMEM((N_HEADS, CHUNK, SUB * D_HEAD), _BF16),
        pltpu.VMEM((N_HEADS, rows, CHUNK), _F32),
    ]
    grid_spec = pltpu.PrefetchScalarGridSpec(
        num_scalar_prefetch=1, grid=(nt + 1,), in_specs=in_specs,
        out_specs=[x_spec, st_out_spec, st_out_spec], scratch_shapes=scratch)
    return pl.pallas_call(
        functools.partial(_mixer_kernel, nt, first),
        grid_spec=grid_spec,
        out_shape=[jax.ShapeDtypeStruct((nb, (nt + 1) * CHUNK, dm), _F32), st_out, st_out],
        input_output_aliases={} if first else {1: 0},
        compiler_params=pltpu.CompilerParams(dimension_semantics=("arbitrary",),
                                             vmem_limit_bytes=VMEM_LIMIT_BYTES),
        name="mixer_first" if first else "mixer",
    )(lv, *xs_in, shg_in, sret_in, norm_w, w_in, lb_logits, hg_norm_w, ret_norm_w, ret_norm_b, w_branch,
      w_out, inv2, tril3, e8)


def _ffn_kernel(nt, last, l_ref, x_ref, cc_in_ref, nw_ref, wup_ref, cw_ref, cb_ref, wd_ref, nfw_ref, *refs):
    if last:
        yp_ref, ys_ref, cc_ref, h_s, acc_s, ext_s, g_s = refs
    else:
        y_ref, cc_ref, h_s, acc_s, ext_s, g_s = refs
    t = pl.program_id(0)
    nb, _, dm = x_ref.shape
    rows = nb * CHUNK
    d_ff = wd_ref.shape[0]
    fb = FFN_BLOCK
    nblk = d_ff // fb

    @pl.when(t == 0)
    def _():
        cc_ref[...] = jnp.zeros(cc_ref.shape, _F32)

    @pl.when(t == nt)
    def _():
        cc_ref[0] = cc_in_ref[...]

    h_s[...] = _rms_rows(x_ref[...].reshape(rows, dm), nw_ref[...]).astype(_BF16)

    def up(i):
        h = h_s[...]
        cols = slice(i * fb, (i + 1) * fb)
        gcols = slice(d_ff + i * fb, d_ff + (i + 1) * fb)
        ext_s[i, :, CONV_PAD:CONV_PAD + CHUNK, :] = _dot(h, wup_ref[:, cols]).reshape(nb, CHUNK, fb)
        g_s[i] = _dot(h, wup_ref[:, gcols])

    def down(i):
        cols = slice(i * fb, (i + 1) * fb)
        ext = ext_s.at[i]
        ext[:, CONV_PAD - 2:CONV_PAD, :] = cc_ref[0, :, :, cols]
        a3 = ext[:, CONV_PAD:CONV_PAD + CHUNK, :]
        cw = cw_ref[:, cols]
        conv = (cb_ref[:, cols] + ext[:, CONV_PAD - 2:CONV_PAD - 2 + CHUNK, :] * cw[0:1, :]
                + ext[:, CONV_PAD - 1:CONV_PAD - 1 + CHUNK, :] * cw[1:2, :] + a3 * cw[2:3, :])
        cc_ref[0, :, :, cols] = a3[:, CHUNK - 2:CHUNK, :]
        u = (_silu(conv).reshape(rows, fb) * g_s[i]).astype(_BF16)
        return _dot(u, wd_ref[cols, :])

    up(0)
    y = None
    for i in range(nblk):
        if i + 1 < nblk:
            up(i + 1)
        part = down(i)
        prev = x_ref[...].reshape(rows, dm) if i == 0 else acc_s[...]
        if i + 1 < nblk:
            acc_s[...] = prev + part
        else:
            y = prev + part

    if last:
        yn = _rms_rows(y, nfw_ref[...]).reshape(nb, CHUNK, dm)

        @pl.when(t < nt)
        def _():
            yp_ref[...] = yn

        @pl.when(t == nt)
        def _():
            ys_ref[...] = yn
    else:
        y_ref[...] = y.reshape(y_ref.shape)


def _ffn_call(nt, last, lv, x, cc_in, norm_w, w_up, cw, cb, wd, nfw):
    nb, _, dm = x.shape
    rows = nb * CHUNK
    d_ff = wd.shape[1]
    x_spec = pl.BlockSpec((nb, CHUNK, dm), lambda t, l: (0, t, 0))
    in_specs = [
        x_spec,
        _layer_spec((None, nb, 2, d_ff)),
        _layer_spec((None, 1, dm)),
        _layer_spec((None, dm, 2 * d_ff)),
        _layer_spec((None, 3, d_ff)),
        _layer_spec((None, 1, d_ff)),
        _layer_spec((None, d_ff, dm)),
        _const_spec((1, dm)),
    ]
    cc_out = jax.ShapeDtypeStruct((2, nb, 2, d_ff), _F32)
    cc_spec = pl.BlockSpec((1, nb, 2, d_ff), lambda t, l: (t // nt, 0, 0, 0))
    if last:
        out_shape = [jax.ShapeDtypeStruct((nb, nt * CHUNK, dm), _F32), jax.ShapeDtypeStruct((nb, CHUNK, dm), _F32), cc_out]
        out_specs = [pl.BlockSpec((nb, CHUNK, dm), lambda t, l: (0, jnp.minimum(t, nt - 1), 0)),
                     pl.BlockSpec((nb, CHUNK, dm), lambda t, l: (0, 0, 0)), cc_spec]
    else:
        out_shape = [jax.ShapeDtypeStruct(x.shape, _F32), cc_out]
        out_specs = [x_spec, cc_spec]
    scratch = [
        pltpu.VMEM((rows, dm), _BF16),
        pltpu.VMEM((rows, dm), _F32),
        pltpu.VMEM((d_ff // FFN_BLOCK, nb, CHUNK + CONV_PAD, FFN_BLOCK), _F32),
        pltpu.VMEM((d_ff // FFN_BLOCK, rows, FFN_BLOCK), _F32),
    ]
    grid_spec = pltpu.PrefetchScalarGridSpec(
        num_scalar_prefetch=1, grid=(nt + 1,), in_specs=in_specs, out_specs=out_specs, scratch_shapes=scratch)
    return pl.pallas_call(
        functools.partial(_ffn_kernel, nt, last),
        grid_spec=grid_spec,
        out_shape=out_shape,
        input_output_aliases={} if last else {1: 0},
        compiler_params=pltpu.CompilerParams(dimension_semantics=("arbitrary",),
                                             vmem_limit_bytes=VMEM_LIMIT_BYTES),
        name="ffn_last" if last else "ffn",
    )(lv, x, cc_in, norm_w, w_up, cw, cb, wd, nfw)


def _constants():
    half = D_HEAD // 2
    inv = ROPE_BASE ** (-jnp.arange(half, dtype=_F32) / half)
    inv2 = jnp.concatenate([inv, inv])[None, :]
    r = np.arange(CHUNK)
    tril = (r[:, None] >= r[None, :]).astype(np.float32)
    tril3 = jnp.asarray(np.concatenate([tril, tril, tril], axis=1), _BF16)
    e8 = np.zeros((SUB * D_HEAD, D_HEAD), np.float32)
    cols = np.arange(CHUNK)
    for s in range(SUB):
        e8[s * D_HEAD:(s + 1) * D_HEAD, cols[cols % SUB == s]] = 1.0
    return inv2, tril3, jnp.asarray(e8, _BF16)


def kernel(x_prompt, x_sample, state_hgrn, state_ret, cache_conv, norm_mix_w, w_in, hg_lb_logits,
           hg_norm_w, ret_norm_w, ret_norm_b, w_branch, w_out, norm_ffn_w, w_up, conv_w, conv_b,
           w_down, norm_final_w):
    nb, tp, dm = x_prompt.shape
    depth = w_in.shape[0]
    d_ff = w_down.shape[1]
    assert x_sample.shape == (nb, CHUNK, dm) and tp % CHUNK == 0 and d_ff % FFN_BLOCK == 0
    nt = tp // CHUNK
    inv2, tril3, e8 = _constants()

    w_in_b = w_in.astype(_BF16)
    w_branch_b = w_branch.astype(_BF16)
    w_out_b = w_out.astype(_BF16)
    w_up_b = w_up.astype(_BF16)
    w_down_b = w_down.astype(_BF16)
    shg_in = jnp.swapaxes(state_hgrn, -1, -2)
    nmw = norm_mix_w[:, None, :]
    nfw_l = norm_ffn_w[:, None, :]
    hgw = hg_norm_w[:, None, :]
    rnw = ret_norm_w[:, None, :]
    rnb = ret_norm_b[:, None, :]
    cb = conv_b[:, None, :]
    nfin = norm_final_w[None, :]

    xs = (x_prompt, x_sample)
    hg_all, ret_all, cc_all = [], [], []
    for l in range(depth):
        lv = jnp.full((1,), l, jnp.int32)
        x, s_hg, s_ret = _mixer_call(nt, lv, xs, shg_in, state_ret, nmw, w_in_b, hg_lb_logits, hgw, rnw, rnb,
                                     w_branch_b, w_out_b, inv2, tril3, e8)
        *xs, s_cc = _ffn_call(nt, l == depth - 1, lv, x, cache_conv, nfw_l, w_up_b, conv_w, cb, w_down_b, nfin)
        hg_all.append(s_hg)
        ret_all.append(s_ret)
        cc_all.append(s_cc)
    y_prompt, y_sample = xs
    hg_all = jnp.swapaxes(jnp.stack(hg_all), -1, -2)
    ret_all = jnp.stack(ret_all)
    cc_all = jnp.stack(cc_all)
    return (y_prompt, y_sample, hg_all[:, 0], ret_all[:, 0], cc_all[:, 0],
            hg_all[:, 1], ret_all[:, 1], cc_all[:, 1])
```

```python
import functools
import math

import numpy as np
import jax
import jax.numpy as jnp
from jax import lax
from jax.experimental import pallas as pl
from jax.experimental.pallas import tpu as pltpu

CHUNK = 64
SUB = 8
LEVELS = (64, 32, 16)
N_SLOT = sum(CHUNK // m for m in LEVELS)
N_HEADS = 4
D_HEAD = 128
D_BRANCH = N_HEADS * D_HEAD
PAST_LEN = 2048
ROPE_BASE = 10000.0
NORM_EPS = 1e-6
GN_EPS = 1e-5
F_FLOOR = 1e-30
MID = CHUNK // 2
MID_SPREAD_MAX = 80.0
FFN_BLOCK = 256
CONV_PAD = 8
PIECE = 256
VMEM_LIMIT_BYTES = 60 * 1024 * 1024

_F32 = jnp.float32
_BF16 = jnp.bfloat16


def _dot(a, b):
    return jnp.dot(a, b, preferred_element_type=_F32)


def _dot_nt(a, b):
    return lax.dot_general(a, b, (((1,), (1,)), ((), ())), preferred_element_type=_F32)


def _dot_tn(a, b):
    return lax.dot_general(a, b, (((0,), (0,)), ((), ())), preferred_element_type=_F32)


def _sigmoid(x):
    return 0.5 * jnp.tanh(0.5 * x) + 0.5


def _silu(x):
    return x * _sigmoid(x)


def _run(thunks):
    for thunk in thunks:
        thunk()


def _rms_rows(x, w):
    ms = jnp.mean(x * x, axis=-1, keepdims=True)
    return x * lax.rsqrt(ms + NORM_EPS) * w


def _factored_scores(qh, kh, bh, cat1, cat2, p_buf, e8_ref):
    rows = qh.shape[0]
    jrow = lax.broadcasted_iota(jnp.int32, (rows, D_HEAD), 0)
    slot = 0
    for m in LEVELS:
        half = m // 2
        bm = bh.reshape(rows // m, m, D_HEAD)
        delta = bm - bm[:, half - 1:half, :]
        is_q = lax.broadcasted_iota(jnp.int32, bm.shape, 1) >= half
        dec = jnp.exp(jnp.minimum(jnp.where(is_q, delta, -delta), 0.0))
        xm = (jnp.where(is_q, qh.reshape(bm.shape), kh.reshape(bm.shape)) * dec).reshape(rows, D_HEAD)
        hidx = jrow >> int(math.log2(half))
        for s in range(rows // m):
            cs = slice(slot * D_HEAD, (slot + 1) * D_HEAD)
            cat1[:, cs] = jnp.where(hidx == 2 * s + 1, xm, 0.0).astype(_BF16)
            cat2[:, cs] = jnp.where(hidx == 2 * s, xm, 0.0).astype(_BF16)
            slot += 1
    rowg = lax.broadcasted_iota(jnp.int32, (rows // SUB, SUB, D_HEAD), 1)
    qg = qh.reshape(rowg.shape)
    kg = kh.reshape(rowg.shape)
    bg = bh.reshape(rowg.shape)
    for s in range(SUB):
        msk = rowg >= s
        dec = jnp.exp(jnp.where(msk, bg - bg[:, s:s + 1, :], 0.0))
        ps = jnp.where(msk, dec * qg * kg[:, s:s + 1, :], 0.0)
        p_buf[:, s * D_HEAD:(s + 1) * D_HEAD] = ps.reshape(rows, D_HEAD).astype(_BF16)
    ad_col = lax.broadcasted_iota(jnp.int32, (rows, D_HEAD), 1)
    in_block = (ad_col // SUB) == (jrow // SUB)
    ad = jnp.where(in_block, _dot(p_buf[...], e8_ref[...]), 0.0)
    return _dot_nt(cat1[...], cat2[...]) + ad[:, 0:CHUNK]


def _mixer_kernel(nt, first, l_ref, *refs):
    if first:
        xp_ref, xs_ref, *refs = refs
    else:
        x_ref, *refs = refs
    (shg_in_ref, sret_in_ref, nw_ref, win_ref, lbl_ref, hgw_ref, rnw_ref, rnb_ref, wbr_ref, wout_ref, inv_ref,
     tril_ref, e8_ref,
     y_ref, shg_ref, sret_ref,
     h_s, q_s, k_s, bc_s, v_s, rq_s, rk_s, rv_s, sga_s, sgr_s, oa_s, ob_s, oag_s, za_s, m_s, gb_s,
     cat1_s, cat2_s, p_s, am_s) = refs
    t = pl.program_id(0)
    layer = l_ref[0]
    nb, _, dm = y_ref.shape
    rows = nb * CHUNK
    seq_rows = [slice(b * CHUNK, (b + 1) * CHUNK) for b in range(nb)]

    def load_x():
        if first:
            return jnp.where(t == nt, xs_ref[...], xp_ref[...]).reshape(rows, dm)
        return x_ref[...].reshape(rows, dm)

    @pl.when(t == 0)
    def _():
        shg_ref[...] = jnp.zeros(shg_ref.shape, _F32)
        sret_ref[...] = jnp.zeros(sret_ref.shape, _F32)

    @pl.when(t == nt)
    def _():
        shg_ref[0] = shg_in_ref[...]
        sret_ref[0] = sret_in_ref[...]

    h_s[...] = _rms_rows(load_x(), nw_ref[...]).astype(_BF16)

    def proj(group, width=D_BRANCH, offset=0):
        c0 = group * D_BRANCH + offset
        return _dot(h_s[...], win_ref[:, c0:c0 + width])

    logits = lbl_ref[...]
    ex = jnp.exp(logits - jnp.max(logits, axis=0, keepdims=True))
    prob = ex / jnp.sum(ex, axis=0, keepdims=True)
    lrow = lax.broadcasted_iota(jnp.int32, prob.shape, 0)
    lb = jnp.sum(jnp.where((lrow >= 1) & (lrow <= layer), prob, 0.0), axis=0, keepdims=True)

    f_gate = lb + (1.0 - lb) * _sigmoid(proj(1))
    q_s[...] = _silu(proj(0))
    for c in range(0, D_BRANCH, PIECE):
        rq_s[:, c:c + PIECE] = proj(4, PIECE, c)
    k_s[...] = 1.0 - f_gate
    log_f = jnp.log(jnp.maximum(f_gate, F_FLOOR))
    lf_hi = log_f.astype(_BF16)
    r1 = log_f - lf_hi.astype(_F32)
    lf_mid = r1.astype(_BF16)
    lf_lo = (r1 - lf_mid.astype(_F32)).astype(_BF16)
    tril3 = tril_ref[...]
    for rs in seq_rows:
        bc_s[rs, :] = _dot(tril3, jnp.concatenate([lf_hi[rs], lf_mid[rs], lf_lo[rs]], axis=0))
    v_s[...] = proj(2).astype(_BF16)

    jrow = lax.broadcasted_iota(jnp.int32, (rows, D_HEAD), 0) & (CHUNK - 1)
    hgw = hgw_ref[...]

    ball = bc_s[...].reshape(nb, CHUNK, D_BRANCH)
    spread = jnp.max(jnp.abs(ball - ball[:, MID - 1:MID, :]))
    mid_ok = spread <= MID_SPREAD_MAX

    @pl.when(t == 0)
    def _():
        am_s[...] = jnp.zeros(am_s.shape, _F32)

    @pl.when(jnp.logical_not(mid_ok))
    def _():
        def seq_body(b, carry):
            rs = pl.ds(pl.multiple_of(b * CHUNK, CHUNK), CHUNK)
            for hd in range(N_HEADS):
                sl = slice(hd * D_HEAD, (hd + 1) * D_HEAD)
                am_s[hd, rs, :] = _factored_scores(q_s[rs, sl], k_s[rs, sl], bc_s[rs, sl], cat1_s.at[hd],
                                                   cat2_s.at[hd], p_s.at[hd], e8_ref)
            return carry

        lax.fori_loop(0, nb, seq_body, 0)

    causal = (lax.broadcasted_iota(jnp.int32, (CHUNK, CHUNK), 0)
              >= lax.broadcasted_iota(jnp.int32, (CHUNK, CHUNK), 1))

    def proj_piece(dst, group, post, c):
        dst[:, c:c + PIECE] = post(proj(group, PIECE, c))

    def to_bf16(v):
        return v.astype(_BF16)

    def identity(v):
        return v

    def gate_b_piece(c):
        gb_s[:, c:c + PIECE] = _sigmoid(proj(8, PIECE, dm + c))

    hg_units = ((rk_s, 5, identity), (rv_s, 6, to_bf16), (sga_s, 3, _silu), (sgr_s, 7, _silu))
    hg_list = ([functools.partial(proj_piece, dst, group, post, c) for dst, group, post in hg_units
                for c in range(0, D_BRANCH, PIECE)]
               + [functools.partial(gate_b_piece, c) for c in range(0, dm, PIECE)])
    per_hg = -(-len(hg_list) // N_HEADS)
    hg_pieces = [hg_list[i * per_hg:(i + 1) * per_hg] for i in range(N_HEADS)]

    def hg_factors(hd):
        sl = slice(hd * D_HEAD, (hd + 1) * D_HEAD)
        qh = q_s[:, sl]
        bh = bc_s[:, sl]
        b3 = bh.reshape(nb, CHUNK, D_HEAD)
        q3 = qh.reshape(b3.shape)
        k3 = k_s[:, sl].reshape(b3.shape)
        blast = b3[:, CHUNK - 1:CHUNK, :]
        bmid = b3[:, MID - 1:MID, :]
        qe = (qh * jnp.exp(bh)).astype(_BF16)
        kd = (k3 * jnp.exp(blast - b3)).reshape(rows, D_HEAD).astype(_BF16)
        qm = (q3 * jnp.exp(b3 - bmid)).reshape(rows, D_HEAD).astype(_BF16)
        km = (k3 * jnp.exp(bmid - b3)).reshape(rows, D_HEAD).astype(_BF16)
        return qe, kd, qm, km, jnp.exp(blast)

    def hg_state_matmuls(hd, factors, pieces):
        qe, kd, qm, km, chunk_dec = factors
        sl = slice(hd * D_HEAD, (hd + 1) * D_HEAD)
        vh = v_s[:, sl]
        states = [shg_ref[0, b, hd] for b in range(nb)]
        amats = [jnp.where(mid_ok, jnp.where(causal, _dot_nt(qm[rs], km[rs]), 0.0), am_s[hd, rs, :])
                 for rs in seq_rows]
        _run(pieces[0])
        outs = [_dot(amats[b].astype(_BF16), vh[rs]) + _dot_nt(qe[rs], states[b].astype(_BF16))
                for b, rs in enumerate(seq_rows)]
        _run(pieces[1])
        for b, rs in enumerate(seq_rows):
            shg_ref[0, b, hd] = states[b] * chunk_dec[b] + _dot_tn(vh[rs], kd[rs])
        oa_s[:, sl] = _rms_rows(jnp.concatenate(outs, axis=0), hgw)

    factors = hg_factors(0)
    for hd in range(N_HEADS):
        mine = hg_pieces[hd]
        _run(mine[0:1])
        upcoming = hg_factors(hd + 1) if hd + 1 < N_HEADS else None
        hg_state_matmuls(hd, factors, (mine[1:2], mine[2:]))
        factors = upcoming

    pos0 = jnp.where(t == nt, PAST_LEN, t * CHUNK)
    posf = (lax.broadcasted_iota(jnp.int32, (CHUNK, D_HEAD), 0) + pos0).astype(_F32)
    ang = posf * inv_ref[...]
    lane = lax.broadcasted_iota(jnp.int32, (CHUNK, D_HEAD), 1)
    cosf = jnp.cos(ang)
    sinf = jnp.where(lane < D_HEAD // 2, -1.0, 1.0) * jnp.sin(ang)
    cos3 = jnp.broadcast_to(cosf[None], (nb, CHUNK, D_HEAD)).reshape(rows, D_HEAD)
    sin3 = jnp.broadcast_to(sinf[None], (nb, CHUNK, D_HEAD)).reshape(rows, D_HEAD)

    def rope(v):
        return v * cos3 + pltpu.roll(v, D_HEAD // 2, 1) * sin3

    def za_piece(c):
        za_s[:, c:c + PIECE] = _dot(oag_s[...], wbr_ref[0, :, c:c + PIECE])

    def gate_a_piece(c):
        m_s[:, c:c + PIECE] = _sigmoid(proj(8, PIECE, c)) * za_s[:, c:c + PIECE]

    oag_s[...] = (oa_s[...] * sga_s[...]).astype(_BF16)
    ret_list = ([functools.partial(za_piece, c) for c in range(0, dm, PIECE)]
                + [functools.partial(gate_a_piece, c) for c in range(0, dm, PIECE)])
    per_head = -(-len(ret_list) // N_HEADS)
    ret_pieces = [ret_list[i * per_head:(i + 1) * per_head] for i in range(N_HEADS)]

    idx_r = jrow.astype(_F32)
    rel = (lax.broadcasted_iota(jnp.int32, (CHUNK, CHUNK), 0)
           - lax.broadcasted_iota(jnp.int32, (CHUNK, CHUNK), 1)).astype(_F32)
    rnw = rnw_ref[...]
    rnb = rnb_ref[...]
    log_gamma = [math.log(1.0 - 2.0 ** (-5.0 - hd)) for hd in range(N_HEADS)]

    def ret_factors(hd):
        sl = slice(hd * D_HEAD, (hd + 1) * D_HEAD)
        lg = log_gamma[hd]
        kf = rope(rk_s[:, sl]) * (D_HEAD ** -0.5)
        qf = rope(rq_s[:, sl])
        kdec = (kf * jnp.exp(lg * (CHUNK - 1.0 - idx_r))).astype(_BF16)
        qdec = (qf * jnp.exp(lg * (idx_r + 1.0))).astype(_BF16)
        dmat = jnp.where(rel >= 0, jnp.exp(lg * jnp.maximum(rel, 0.0)), 0.0)
        return qf.astype(_BF16), kf.astype(_BF16), kdec, qdec, dmat

    def ret_state_matmuls(hd, factors, pieces):
        qb, kb, kdec, qdec, dmat = factors
        sl = slice(hd * D_HEAD, (hd + 1) * D_HEAD)
        vh = rv_s[:, sl]
        states = [sret_ref[0, b, hd] for b in range(nb)]
        amats = [_dot_nt(qb[rs], kb[rs]) * dmat for rs in seq_rows]
        _run(pieces[0])
        outs = [_dot(amats[b].astype(_BF16), vh[rs]) + _dot(qdec[rs], states[b].astype(_BF16))
                for b, rs in enumerate(seq_rows)]
        _run(pieces[1])
        for b, rs in enumerate(seq_rows):
            sret_ref[0, b, hd] = math.exp(log_gamma[hd] * CHUNK) * states[b] + _dot_tn(kdec[rs], vh[rs])
        o = jnp.concatenate(outs, axis=0)
        mu = jnp.mean(o, axis=-1, keepdims=True)
        oc = o - mu
        var = jnp.mean(oc * oc, axis=-1, keepdims=True)
        ob_s[:, sl] = oc * lax.rsqrt(var + GN_EPS) * rnw[:, sl] + rnb[:, sl]

    factors = ret_factors(0)
    for hd in range(N_HEADS):
        mine = ret_pieces[hd]
        _run(mine[0:1])
        upcoming = ret_factors(hd + 1) if hd + 1 < N_HEADS else None
        ret_state_matmuls(hd, factors, (mine[1:2], mine[2:]))
        factors = upcoming

    z_b = _dot((ob_s[...] * sgr_s[...]).astype(_BF16), wbr_ref[1])
    merged = m_s[...] + gb_s[...] * z_b
    y = load_x() + _dot(merged.astype(_BF16), wout_ref[...])
    y_ref[...] = y.reshape(y_ref.shape)


def _layer_spec(block):
    zeros = (0,) * (len(block) - 1)
    return pl.BlockSpec(block, lambda t, l: (l[0],) + zeros, pipeline_mode=pl.Buffered(1))


def _const_spec(block):
    zeros = (0,) * len(block)
    return pl.BlockSpec(block, lambda t, l: zeros, pipeline_mode=pl.Buffered(1))


def _mixer_call(nt, lv, xs_in, shg_in, sret_in, norm_w, w_in, lb_logits, hg_norm_w, ret_norm_w,
                ret_norm_b, w_branch, w_out, inv2, tril3, e8):
    first = len(xs_in) == 2
    nb, _, dm = xs_in[0].shape
    depth = w_in.shape[0]
    rows = nb * CHUNK
    d_in = w_in.shape[2]
    st_block = (None, nb, N_HEADS, D_HEAD, D_HEAD)
    x_spec = pl.BlockSpec((nb, CHUNK, dm), lambda t, l: (0, t, 0))
    if first:
        x_specs = [pl.BlockSpec((nb, CHUNK, dm), lambda t, l: (0, jnp.minimum(t, nt - 1), 0)),
                   _const_spec((nb, CHUNK, dm))]
    else:
        x_specs = [x_spec]
    in_specs = x_specs + [
        _layer_spec(st_block),
        _layer_spec(st_block),
        _layer_spec((None, 1, dm)),
        _layer_spec((None, dm, d_in)),
        _const_spec((depth, D_BRANCH)),
        _layer_spec((None, 1, D_HEAD)),
        _layer_spec((None, 1, D_BRANCH)),
        _layer_spec((None, 1, D_BRANCH)),
        _layer_spec((None, 2, D_BRANCH, dm)),
        _layer_spec((None, dm, dm)),
        _const_spec((1, D_HEAD)),
        _const_spec((CHUNK, 3 * CHUNK)),
        _const_spec((SUB * D_HEAD, D_HEAD)),
    ]
    st_out = jax.ShapeDtypeStruct((2, nb, N_HEADS, D_HEAD, D_HEAD), _F32)
    st_out_spec = pl.BlockSpec((1, nb, N_HEADS, D_HEAD, D_HEAD), lambda t, l: (t // nt, 0, 0, 0, 0))
    scratch = [
        pltpu.VMEM((rows, dm), _BF16),
        pltpu.VMEM((rows, D_BRANCH), _F32),
        pltpu.VMEM((rows, D_BRANCH), _F32),
        pltpu.VMEM((rows, D_BRANCH), _F32),
        pltpu.VMEM((rows, D_BRANCH), _BF16),
        pltpu.VMEM((rows, D_BRANCH), _F32),
        pltpu.VMEM((rows, D_BRANCH), _F32),
        pltpu.VMEM((rows, D_BRANCH), _BF16),
        pltpu.VMEM((rows, D_BRANCH), _F32),
        pltpu.VMEM((rows, D_BRANCH), _F32),
        pltpu.VMEM((rows, D_BRANCH), _F32),
        pltpu.VMEM((rows, D_BRANCH), _F32),
        pltpu.VMEM((rows, D_BRANCH), _BF16),
        pltpu.VMEM((rows, dm), _F32),
        pltpu.VMEM((rows, dm), _F32),
        pltpu.VMEM((rows, dm), _F32),
        pltpu.VMEM((N_HEADS, CHUNK, N_SLOT * D_HEAD), _BF16),
        pltpu.VMEM((N_HEADS, CHUNK, N_SLOT * D_HEAD), _BF16),
        pltpu.VMEM((N_HEADS, CHUNK, SUB * D_HEAD), _BF16),
        pltpu.VMEM((N_HEADS, rows, CHUNK), _F32),
    ]
    grid_spec = pltpu.PrefetchScalarGridSpec(
        num_scalar_prefetch=1, grid=(nt + 1,), in_specs=in_specs,
        out_specs=[x_spec, st_out_spec, st_out_spec], scratch_shapes=scratch)
    return pl.pallas_call(
        functools.partial(_mixer_kernel, nt, first),
        grid_spec=grid_spec,
        out_shape=[jax.ShapeDtypeStruct((nb, (nt + 1) * CHUNK, dm), _F32), st_out, st_out],
        input_output_aliases={} if first else {1: 0},
        compiler_params=pltpu.CompilerParams(dimension_semantics=("arbitrary",),
                                             vmem_limit_bytes=VMEM_LIMIT_BYTES),
        name="mixer_first" if first else "mixer",
    )(lv, *xs_in, shg_in, sret_in, norm_w, w_in, lb_logits, hg_norm_w, ret_norm_w, ret_norm_b, w_branch,
      w_out, inv2, tril3, e8)


def _ffn_kernel(nt, last, l_ref, x_ref, cc_in_ref, nw_ref, wup_ref, cw_ref, cb_ref, wd_ref, nfw_ref, *refs):
    if last:
        yp_ref, ys_ref, cc_ref, h_s, acc_s, ext_s, g_s = refs
    else:
        y_ref, cc_ref, h_s, acc_s, ext_s, g_s = refs
    t = pl.program_id(0)
    nb, _, dm = x_ref.shape
    rows = nb * CHUNK
    d_ff = wd_ref.shape[0]
    fb = FFN_BLOCK
    nblk = d_ff // fb

    @pl.when(t == 0)
    def _():
        cc_ref[...] = jnp.zeros(cc_ref.shape, _F32)

    @pl.when(t == nt)
    def _():
        cc_ref[0] = cc_in_ref[...]

    h_s[...] = _rms_rows(x_ref[...].reshape(rows, dm), nw_ref[...]).astype(_BF16)

    def up(i):
        h = h_s[...]
        cols = slice(i * fb, (i + 1) * fb)
        gcols = slice(d_ff + i * fb, d_ff + (i + 1) * fb)
        ext_s[i % 2, :, CONV_PAD:CONV_PAD + CHUNK, :] = _dot(h, wup_ref[:, cols]).reshape(nb, CHUNK, fb)
        g_s[i % 2] = _dot(h, wup_ref[:, gcols])

    def down(i):
        cols = slice(i * fb, (i + 1) * fb)
        ext = ext_s.at[i % 2]
        ext[:, CONV_PAD - 2:CONV_PAD, :] = cc_ref[0, :, :, cols]
        a3 = ext[:, CONV_PAD:CONV_PAD + CHUNK, :]
        cw = cw_ref[:, cols]
        conv = (cb_ref[:, cols] + ext[:, CONV_PAD - 2:CONV_PAD - 2 + CHUNK, :] * cw[0:1, :]
                + ext[:, CONV_PAD - 1:CONV_PAD - 1 + CHUNK, :] * cw[1:2, :] + a3 * cw[2:3, :])
        cc_ref[0, :, :, cols] = a3[:, CHUNK - 2:CHUNK, :]
        u = (_silu(conv).reshape(rows, fb) * g_s[i % 2]).astype(_BF16)
        return _dot(u, wd_ref[cols, :])

    up(0)
    y = None
    for i in range(nblk):
        if i + 1 < nblk:
            up(i + 1)
        part = down(i)
        prev = x_ref[...].reshape(rows, dm) if i == 0 else acc_s[...]
        if i + 1 < nblk:
            acc_s[...] = prev + part
        else:
            y = prev + part

    if last:
        yn = _rms_rows(y, nfw_ref[...]).reshape(nb, CHUNK, dm)

        @pl.when(t < nt)
        def _():
            yp_ref[...] = yn

        @pl.when(t == nt)
        def _():
            ys_ref[...] = yn
    else:
        y_ref[...] = y.reshape(y_ref.shape)


def _ffn_call(nt, last, lv, x, cc_in, norm_w, w_up, cw, cb, wd, nfw):
    nb, _, dm = x.shape
    rows = nb * CHUNK
    d_ff = wd.shape[1]
    x_spec = pl.BlockSpec((nb, CHUNK, dm), lambda t, l: (0, t, 0))
    in_specs = [
        x_spec,
        _layer_spec((None, nb, 2, d_ff)),
        _layer_spec((None, 1, dm)),
        _layer_spec((None, dm, 2 * d_ff)),
        _layer_spec((None, 3, d_ff)),
        _layer_spec((None, 1, d_ff)),
        _layer_spec((None, d_ff, dm)),
        _const_spec((1, dm)),
    ]
    cc_out = jax.ShapeDtypeStruct((2, nb, 2, d_ff), _F32)
    cc_spec = pl.BlockSpec((1, nb, 2, d_ff), lambda t, l: (t // nt, 0, 0, 0))
    if last:
        out_shape = [jax.ShapeDtypeStruct((nb, nt * CHUNK, dm), _F32), jax.ShapeDtypeStruct((nb, CHUNK, dm), _F32), cc_out]
        out_specs = [pl.BlockSpec((nb, CHUNK, dm), lambda t, l: (0, jnp.minimum(t, nt - 1), 0)),
                     pl.BlockSpec((nb, CHUNK, dm), lambda t, l: (0, 0, 0)), cc_spec]
    else:
        out_shape = [jax.ShapeDtypeStruct(x.shape, _F32), cc_out]
        out_specs = [x_spec, cc_spec]
    scratch = [
        pltpu.VMEM((rows, dm), _BF16),
        pltpu.VMEM((rows, dm), _F32),
        pltpu.VMEM((2, nb, CHUNK + CONV_PAD, FFN_BLOCK), _F32),
        pltpu.VMEM((2, rows, FFN_BLOCK), _F32),
    ]
    grid_spec = pltpu.PrefetchScalarGridSpec(
        num_scalar_prefetch=1, grid=(nt + 1,), in_specs=in_specs, out_specs=out_specs, scratch_shapes=scratch)
    return pl.pallas_call(
        functools.partial(_ffn_kernel, nt, last),
        grid_spec=grid_spec,
        out_shape=out_shape,
        input_output_aliases={} if last else {1: 0},
        compiler_params=pltpu.CompilerParams(dimension_semantics=("arbitrary",),
                                             vmem_limit_bytes=VMEM_LIMIT_BYTES),
        name="ffn_last" if last else "ffn",
    )(lv, x, cc_in, norm_w, w_up, cw, cb, wd, nfw)


def _constants():
    half = D_HEAD // 2
    inv = ROPE_BASE ** (-jnp.arange(half, dtype=_F32) / half)
    inv2 = jnp.concatenate([inv, inv])[None, :]
    r = np.arange(CHUNK)
    tril = (r[:, None] >= r[None, :]).astype(np.float32)
    tril3 = jnp.asarray(np.concatenate([tril, tril, tril], axis=1), _BF16)
    e8 = np.zeros((SUB * D_HEAD, D_HEAD), np.float32)
    cols = np.arange(CHUNK)
    for s in range(SUB):
        e8[s * D_HEAD:(s + 1) * D_HEAD, cols[cols % SUB == s]] = 1.0
    return inv2, tril3, jnp.asarray(e8, _BF16)


def kernel(x_prompt, x_sample, state_hgrn, state_ret, cache_conv, norm_mix_w, w_in, hg_lb_logits,
           hg_norm_w, ret_norm_w, ret_norm_b, w_branch, w_out, norm_ffn_w, w_up, conv_w, conv_b,
           w_down, norm_final_w):
    nb, tp, dm = x_prompt.shape
    depth = w_in.shape[0]
    d_ff = w_down.shape[1]
    assert x_sample.shape == (nb, CHUNK, dm) and tp % CHUNK == 0 and d_ff % FFN_BLOCK == 0
    nt = tp // CHUNK
    inv2, tril3, e8 = _constants()

    w_in_b = w_in.astype(_BF16)
    w_branch_b = w_branch.astype(_BF16)
    w_out_b = w_out.astype(_BF16)
    w_up_b = w_up.astype(_BF16)
    w_down_b = w_down.astype(_BF16)
    shg_in = jnp.swapaxes(state_hgrn, -1, -2)
    nmw = norm_mix_w[:, None, :]
    nfw_l = norm_ffn_w[:, None, :]
    hgw = hg_norm_w[:, None, :]
    rnw = ret_norm_w[:, None, :]
    rnb = ret_norm_b[:, None, :]
    cb = conv_b[:, None, :]
    nfin = norm_final_w[None, :]

    xs = (x_prompt, x_sample)
    hg_all, ret_all, cc_all = [], [], []
    for l in range(depth):
        lv = jnp.full((1,), l, jnp.int32)
        x, s_hg, s_ret = _mixer_call(nt, lv, xs, shg_in, state_ret, nmw, w_in_b, hg_lb_logits, hgw, rnw, rnb,
                                     w_branch_b, w_out_b, inv2, tril3, e8)
        *xs, s_cc = _ffn_call(nt, l == depth - 1, lv, x, cache_conv, nfw_l, w_up_b, conv_w, cb, w_down_b, nfin)
        hg_all.append(s_hg)
        ret_all.append(s_ret)
        cc_all.append(s_cc)
    y_prompt, y_sample = xs
    hg_all = jnp.swapaxes(jnp.stack(hg_all), -1, -2)
    ret_all = jnp.stack(ret_all)
    cc_all = jnp.stack(cc_all)
    return (y_prompt, y_sample, hg_all[:, 0], ret_all[:, 0], cc_all[:, 0],
            hg_all[:, 1], ret_all[:, 1], cc_all[:, 1])
```

```python
import functools
import math

import numpy as np
import jax
import jax.numpy as jnp
from jax import lax
from jax.experimental import pallas as pl
from jax.experimental.pallas import tpu as pltpu

CHUNK = 64
SUB = 8
LEVELS = (64, 32, 16)
N_SLOT = sum(CHUNK // m for m in LEVELS)
N_HEADS = 4
D_HEAD = 128
D_BRANCH = N_HEADS * D_HEAD
PAST_LEN = 2048
ROPE_BASE = 10000.0
NORM_EPS = 1e-6
GN_EPS = 1e-5
F_FLOOR = 1e-30
MID = CHUNK // 2
MID_SPREAD_MAX = 80.0 / math.log(2.0)
FFN_BLOCK = 256
CONV_PAD = 8
PIECE = 256
VMEM_LIMIT_BYTES = 60 * 1024 * 1024

_F32 = jnp.float32
_BF16 = jnp.bfloat16


def _dot(a, b):
    return jnp.dot(a, b, preferred_element_type=_F32)


def _dot_nt(a, b):
    return lax.dot_general(a, b, (((1,), (1,)), ((), ())), preferred_element_type=_F32)


def _dot_tn(a, b):
    return lax.dot_general(a, b, (((0,), (0,)), ((), ())), preferred_element_type=_F32)


def _sigmoid(x):
    return 0.5 * jnp.tanh(0.5 * x) + 0.5


def _silu(x):
    return x * _sigmoid(x)


def _run(thunks):
    for thunk in thunks:
        thunk()


def _rms_rows(x, w):
    ms = jnp.mean(x * x, axis=-1, keepdims=True)
    return x * lax.rsqrt(ms + NORM_EPS) * w


def _factored_scores(qh, kh, bh, cat1, cat2, p_buf, e8_ref):
    rows = qh.shape[0]
    jrow = lax.broadcasted_iota(jnp.int32, (rows, D_HEAD), 0)
    slot = 0
    for m in LEVELS:
        half = m // 2
        bm = bh.reshape(rows // m, m, D_HEAD)
        delta = bm - bm[:, half - 1:half, :]
        is_q = lax.broadcasted_iota(jnp.int32, bm.shape, 1) >= half
        dec = jnp.exp2(jnp.minimum(jnp.where(is_q, delta, -delta), 0.0))
        xm = (jnp.where(is_q, qh.reshape(bm.shape), kh.reshape(bm.shape)) * dec).reshape(rows, D_HEAD)
        hidx = jrow >> int(math.log2(half))
        for s in range(rows // m):
            cs = slice(slot * D_HEAD, (slot + 1) * D_HEAD)
            cat1[:, cs] = jnp.where(hidx == 2 * s + 1, xm, 0.0).astype(_BF16)
            cat2[:, cs] = jnp.where(hidx == 2 * s, xm, 0.0).astype(_BF16)
            slot += 1
    rowg = lax.broadcasted_iota(jnp.int32, (rows // SUB, SUB, D_HEAD), 1)
    qg = qh.reshape(rowg.shape)
    kg = kh.reshape(rowg.shape)
    bg = bh.reshape(rowg.shape)
    for s in range(SUB):
        msk = rowg >= s
        dec = jnp.exp2(jnp.where(msk, bg - bg[:, s:s + 1, :], 0.0))
        ps = jnp.where(msk, dec * qg * kg[:, s:s + 1, :], 0.0)
        p_buf[:, s * D_HEAD:(s + 1) * D_HEAD] = ps.reshape(rows, D_HEAD).astype(_BF16)
    ad_col = lax.broadcasted_iota(jnp.int32, (rows, D_HEAD), 1)
    in_block = (ad_col // SUB) == (jrow // SUB)
    ad = jnp.where(in_block, _dot(p_buf[...], e8_ref[...]), 0.0)
    return _dot_nt(cat1[...], cat2[...]) + ad[:, 0:CHUNK]


def _mixer_kernel(nt, first, l_ref, *refs):
    if first:
        xp_ref, xs_ref, *refs = refs
    else:
        x_ref, *refs = refs
    (shg_in_ref, sret_in_ref, nw_ref, win_ref, lbl_ref, hgw_ref, rnw_ref, rnb_ref, wbr_ref, wout_ref, inv_ref,
     tril_ref, e8_ref,
     y_ref, shg_ref, sret_ref,
     h_s, q_s, k_s, bc_s, v_s, rq_s, rk_s, rv_s, sga_s, sgr_s, oa_s, ob_s, oag_s, za_s, m_s, gb_s,
     cat1_s, cat2_s, p_s, am_s) = refs
    t = pl.program_id(0)
    layer = l_ref[0]
    nb, _, dm = y_ref.shape
    rows = nb * CHUNK
    seq_rows = [slice(b * CHUNK, (b + 1) * CHUNK) for b in range(nb)]

    def load_x():
        if first:
            return jnp.where(t == nt, xs_ref[...], xp_ref[...]).reshape(rows, dm)
        return x_ref[...].reshape(rows, dm)

    @pl.when(t == 0)
    def _():
        shg_ref[...] = jnp.zeros(shg_ref.shape, _F32)
        sret_ref[...] = jnp.zeros(sret_ref.shape, _F32)

    @pl.when(t == nt)
    def _():
        shg_ref[0] = shg_in_ref[...]
        sret_ref[0] = sret_in_ref[...]

    h_s[...] = _rms_rows(load_x(), nw_ref[...]).astype(_BF16)

    def proj(group, width=D_BRANCH, offset=0):
        c0 = group * D_BRANCH + offset
        return _dot(h_s[...], win_ref[:, c0:c0 + width])

    logits = lbl_ref[...]
    ex = jnp.exp(logits - jnp.max(logits, axis=0, keepdims=True))
    prob = ex / jnp.sum(ex, axis=0, keepdims=True)
    lrow = lax.broadcasted_iota(jnp.int32, prob.shape, 0)
    lb = jnp.sum(jnp.where((lrow >= 1) & (lrow <= layer), prob, 0.0), axis=0, keepdims=True)

    f_gate = lb + (1.0 - lb) * _sigmoid(proj(1))
    q_s[...] = _silu(proj(0))
    for c in range(0, D_BRANCH, PIECE):
        rq_s[:, c:c + PIECE] = proj(4, PIECE, c)
    k_s[...] = 1.0 - f_gate
    log_f = jnp.log2(jnp.maximum(f_gate, F_FLOOR))
    lf_hi = log_f.astype(_BF16)
    r1 = log_f - lf_hi.astype(_F32)
    lf_mid = r1.astype(_BF16)
    lf_lo = (r1 - lf_mid.astype(_F32)).astype(_BF16)
    tril3 = tril_ref[...]
    for rs in seq_rows:
        bc_s[rs, :] = _dot(tril3, jnp.concatenate([lf_hi[rs], lf_mid[rs], lf_lo[rs]], axis=0))
    v_s[...] = proj(2).astype(_BF16)

    jrow = lax.broadcasted_iota(jnp.int32, (rows, D_HEAD), 0) & (CHUNK - 1)
    hgw = hgw_ref[...]

    ball = bc_s[...].reshape(nb, CHUNK, D_BRANCH)
    spread = jnp.max(jnp.abs(ball - ball[:, MID - 1:MID, :]))
    mid_ok = spread <= MID_SPREAD_MAX

    @pl.when(t == 0)
    def _():
        am_s[...] = jnp.zeros(am_s.shape, _F32)

    @pl.when(jnp.logical_not(mid_ok))
    def _():
        def seq_body(b, carry):
            rs = pl.ds(pl.multiple_of(b * CHUNK, CHUNK), CHUNK)
            for hd in range(N_HEADS):
                sl = slice(hd * D_HEAD, (hd + 1) * D_HEAD)
                am_s[hd, rs, :] = _factored_scores(q_s[rs, sl], k_s[rs, sl], bc_s[rs, sl], cat1_s.at[hd],
                                                   cat2_s.at[hd], p_s.at[hd], e8_ref)
            return carry

        lax.fori_loop(0, nb, seq_body, 0)

    causal = (lax.broadcasted_iota(jnp.int32, (CHUNK, CHUNK), 0)
              >= lax.broadcasted_iota(jnp.int32, (CHUNK, CHUNK), 1))

    def proj_piece(dst, group, post, c):
        dst[:, c:c + PIECE] = post(proj(group, PIECE, c))

    def to_bf16(v):
        return v.astype(_BF16)

    def identity(v):
        return v

    def gate_b_piece(c):
        gb_s[:, c:c + PIECE] = _sigmoid(proj(8, PIECE, dm + c))

    hg_units = ((rk_s, 5, identity), (rv_s, 6, to_bf16), (sga_s, 3, _silu), (sgr_s, 7, _silu))
    hg_list = ([functools.partial(proj_piece, dst, group, post, c) for dst, group, post in hg_units
                for c in range(0, D_BRANCH, PIECE)]
               + [functools.partial(gate_b_piece, c) for c in range(0, dm, PIECE)])
    per_hg = -(-len(hg_list) // N_HEADS)
    hg_pieces = [hg_list[i * per_hg:(i + 1) * per_hg] for i in range(N_HEADS)]

    def hg_factors(hd):
        sl = slice(hd * D_HEAD, (hd + 1) * D_HEAD)
        qh = q_s[:, sl]
        bh = bc_s[:, sl]
        b3 = bh.reshape(nb, CHUNK, D_HEAD)
        q3 = qh.reshape(b3.shape)
        k3 = k_s[:, sl].reshape(b3.shape)
        blast = b3[:, CHUNK - 1:CHUNK, :]
        bmid = b3[:, MID - 1:MID, :]
        qe = (qh * jnp.exp2(bh)).astype(_BF16)
        kd = (k3 * jnp.exp2(blast - b3)).reshape(rows, D_HEAD).astype(_BF16)
        qm = (q3 * jnp.exp2(b3 - bmid)).reshape(rows, D_HEAD).astype(_BF16)
        km = (k3 * jnp.exp2(bmid - b3)).reshape(rows, D_HEAD).astype(_BF16)
        return qe, kd, qm, km, jnp.exp2(blast)

    def hg_state_matmuls(hd, factors, pieces):
        qe, kd, qm, km, chunk_dec = factors
        sl = slice(hd * D_HEAD, (hd + 1) * D_HEAD)
        vh = v_s[:, sl]
        states = [shg_ref[0, b, hd] for b in range(nb)]
        amats = [jnp.where(mid_ok, jnp.where(causal, _dot_nt(qm[rs], km[rs]), 0.0), am_s[hd, rs, :])
                 for rs in seq_rows]
        _run(pieces[0])
        outs = [_dot(amats[b].astype(_BF16), vh[rs]) + _dot_nt(qe[rs], states[b].astype(_BF16))
                for b, rs in enumerate(seq_rows)]
        _run(pieces[1])
        for b, rs in enumerate(seq_rows):
            shg_ref[0, b, hd] = states[b] * chunk_dec[b] + _dot_tn(vh[rs], kd[rs])
        oa_s[:, sl] = _rms_rows(jnp.concatenate(outs, axis=0), hgw)

    factors = hg_factors(0)
    for hd in range(N_HEADS):
        mine = hg_pieces[hd]
        _run(mine[0:1])
        upcoming = hg_factors(hd + 1) if hd + 1 < N_HEADS else None
        hg_state_matmuls(hd, factors, (mine[1:2], mine[2:]))
        factors = upcoming

    pos0 = jnp.where(t == nt, PAST_LEN, t * CHUNK)
    posf = (lax.broadcasted_iota(jnp.int32, (CHUNK, D_HEAD), 0) + pos0).astype(_F32)
    ang = posf * inv_ref[...]
    lane = lax.broadcasted_iota(jnp.int32, (CHUNK, D_HEAD), 1)
    cosf = jnp.cos(ang)
    sinf = jnp.where(lane < D_HEAD // 2, -1.0, 1.0) * jnp.sin(ang)
    cos3 = jnp.broadcast_to(cosf[None], (nb, CHUNK, D_HEAD)).reshape(rows, D_HEAD)
    sin3 = jnp.broadcast_to(sinf[None], (nb, CHUNK, D_HEAD)).reshape(rows, D_HEAD)

    def rope(v):
        return v * cos3 + pltpu.roll(v, D_HEAD // 2, 1) * sin3

    def za_piece(c):
        za_s[:, c:c + PIECE] = _dot(oag_s[...], wbr_ref[0, :, c:c + PIECE])

    def gate_a_piece(c):
        m_s[:, c:c + PIECE] = _sigmoid(proj(8, PIECE, c)) * za_s[:, c:c + PIECE]

    oag_s[...] = (oa_s[...] * sga_s[...]).astype(_BF16)
    ret_list = ([functools.partial(za_piece, c) for c in range(0, dm, PIECE)]
                + [functools.partial(gate_a_piece, c) for c in range(0, dm, PIECE)])
    per_head = -(-len(ret_list) // N_HEADS)
    ret_pieces = [ret_list[i * per_head:(i + 1) * per_head] for i in range(N_HEADS)]

    idx_r = jrow.astype(_F32)
    rel = (lax.broadcasted_iota(jnp.int32, (CHUNK, CHUNK), 0)
           - lax.broadcasted_iota(jnp.int32, (CHUNK, CHUNK), 1)).astype(_F32)
    rnw = rnw_ref[...]
    rnb = rnb_ref[...]
    log_gamma = [math.log(1.0 - 2.0 ** (-5.0 - hd)) for hd in range(N_HEADS)]

    def ret_factors(hd):
        sl = slice(hd * D_HEAD, (hd + 1) * D_HEAD)
        lg = log_gamma[hd] / math.log(2.0)
        kf = rope(rk_s[:, sl]) * (D_HEAD ** -0.5)
        qf = rope(rq_s[:, sl])
        kdec = (kf * jnp.exp2(lg * (CHUNK - 1.0 - idx_r))).astype(_BF16)
        qdec = (qf * jnp.exp2(lg * (idx_r + 1.0))).astype(_BF16)
        dmat = jnp.where(rel >= 0, jnp.exp2(lg * jnp.maximum(rel, 0.0)), 0.0)
        return qf.astype(_BF16), kf.astype(_BF16), kdec, qdec, dmat

    def ret_state_matmuls(hd, factors, pieces):
        qb, kb, kdec, qdec, dmat = factors
        sl = slice(hd * D_HEAD, (hd + 1) * D_HEAD)
        vh = rv_s[:, sl]
        states = [sret_ref[0, b, hd] for b in range(nb)]
        amats = [_dot_nt(qb[rs], kb[rs]) * dmat for rs in seq_rows]
        _run(pieces[0])
        outs = [_dot(amats[b].astype(_BF16), vh[rs]) + _dot(qdec[rs], states[b].astype(_BF16))
                for b, rs in enumerate(seq_rows)]
        _run(pieces[1])
        for b, rs in enumerate(seq_rows):
            sret_ref[0, b, hd] = math.exp(log_gamma[hd] * CHUNK) * states[b] + _dot_tn(kdec[rs], vh[rs])
        o = jnp.concatenate(outs, axis=0)
        mu = jnp.mean(o, axis=-1, keepdims=True)
        oc = o - mu
        var = jnp.mean(oc * oc, axis=-1, keepdims=True)
        ob_s[:, sl] = oc * lax.rsqrt(var + GN_EPS) * rnw[:, sl] + rnb[:, sl]

    factors = ret_factors(0)
    for hd in range(N_HEADS):
        mine = ret_pieces[hd]
        _run(mine[0:1])
        upcoming = ret_factors(hd + 1) if hd + 1 < N_HEADS else None
        ret_state_matmuls(hd, factors, (mine[1:2], mine[2:]))
        factors = upcoming

    z_b = _dot((ob_s[...] * sgr_s[...]).astype(_BF16), wbr_ref[1])
    merged = m_s[...] + gb_s[...] * z_b
    y = load_x() + _dot(merged.astype(_BF16), wout_ref[...])
    y_ref[...] = y.reshape(y_ref.shape)


def _layer_spec(block):
    zeros = (0,) * (len(block) - 1)
    return pl.BlockSpec(block, lambda t, l: (l[0],) + zeros, pipeline_mode=pl.Buffered(1))


def _const_spec(block):
    zeros = (0,) * len(block)
    return pl.BlockSpec(block, lambda t, l: zeros, pipeline_mode=pl.Buffered(1))


def _mixer_call(nt, lv, xs_in, shg_in, sret_in, norm_w, w_in, lb_logits, hg_norm_w, ret_norm_w,
                ret_norm_b, w_branch, w_out, inv2, tril3, e8):
    first = len(xs_in) == 2
    nb, _, dm = xs_in[0].shape
    depth = w_in.shape[0]
    rows = nb * CHUNK
    d_in = w_in.shape[2]
    st_block = (None, nb, N_HEADS, D_HEAD, D_HEAD)
    x_spec = pl.BlockSpec((nb, CHUNK, dm), lambda t, l: (0, t, 0))
    if first:
        x_specs = [pl.BlockSpec((nb, CHUNK, dm), lambda t, l: (0, jnp.minimum(t, nt - 1), 0)),
                   _const_spec((nb, CHUNK, dm))]
    else:
        x_specs = [x_spec]
    in_specs = x_specs + [
        _layer_spec(st_block),
        _layer_spec(st_block),
        _layer_spec((None, 1, dm)),
        _layer_spec((None, dm, d_in)),
        _const_spec((depth, D_BRANCH)),
        _layer_spec((None, 1, D_HEAD)),
        _layer_spec((None, 1, D_BRANCH)),
        _layer_spec((None, 1, D_BRANCH)),
        _layer_spec((None, 2, D_BRANCH, dm)),
        _layer_spec((None, dm, dm)),
        _const_spec((1, D_HEAD)),
        _const_spec((CHUNK, 3 * CHUNK)),
        _const_spec((SUB * D_HEAD, D_HEAD)),
    ]
    st_out = jax.ShapeDtypeStruct((2, nb, N_HEADS, D_HEAD, D_HEAD), _F32)
    st_out_spec = pl.BlockSpec((1, nb, N_HEADS, D_HEAD, D_HEAD), lambda t, l: (t // nt, 0, 0, 0, 0))
    scratch = [
        pltpu.VMEM((rows, dm), _BF16),
        pltpu.VMEM((rows, D_BRANCH), _F32),
        pltpu.VMEM((rows, D_BRANCH), _F32),
        pltpu.VMEM((rows, D_BRANCH), _F32),
        pltpu.VMEM((rows, D_BRANCH), _BF16),
        pltpu.VMEM((rows, D_BRANCH), _F32),
        pltpu.VMEM((rows, D_BRANCH), _F32),
        pltpu.VMEM((rows, D_BRANCH), _BF16),
        pltpu.VMEM((rows, D_BRANCH), _F32),
        pltpu.VMEM((rows, D_BRANCH), _F32),
        pltpu.VMEM((rows, D_BRANCH), _F32),
        pltpu.VMEM((rows, D_BRANCH), _F32),
        pltpu.VMEM((rows, D_BRANCH), _BF16),
        pltpu.VMEM((rows, dm), _F32),
        pltpu.VMEM((rows, dm), _F32),
        pltpu.VMEM((rows, dm), _F32),
        pltpu.VMEM((N_HEADS, CHUNK, N_SLOT * D_HEAD), _BF16),
        pltpu.VMEM((N_HEADS, CHUNK, N_SLOT * D_HEAD), _BF16),
        pltpu.VMEM((N_HEADS, CHUNK, SUB * D_HEAD), _BF16),
        pltpu.VMEM((N_HEADS, rows, CHUNK), _F32),
    ]
    grid_spec = pltpu.PrefetchScalarGridSpec(
        num_scalar_prefetch=1, grid=(nt + 1,), in_specs=in_specs,
        out_specs=[x_spec, st_out_spec, st_out_spec], scratch_shapes=scratch)
    return pl.pallas_call(
        functools.partial(_mixer_kernel, nt, first),
        grid_spec=grid_spec,
        out_shape=[jax.ShapeDtypeStruct((nb, (nt + 1) * CHUNK, dm), _F32), st_out, st_out],
        input_output_aliases={} if first else {1: 0},
        compiler_params=pltpu.CompilerParams(dimension_semantics=("arbitrary",),
                                             vmem_limit_bytes=VMEM_LIMIT_BYTES),
        name="mixer_first" if first else "mixer",
    )(lv, *xs_in, shg_in, sret_in, norm_w, w_in, lb_logits, hg_norm_w, ret_norm_w, ret_norm_b, w_branch,
      w_out, inv2, tril3, e8)


def _ffn_kernel(nt, last, l_ref, x_ref, cc_in_ref, nw_ref, wup_ref, cw_ref, cb_ref, wd_ref, nfw_ref, *refs):
    if last:
        yp_ref, ys_ref, cc_ref, h_s, acc_s, ext_s, g_s = refs
    else:
        y_ref, cc_ref, h_s, acc_s, ext_s, g_s = refs
    t = pl.program_id(0)
    nb, _, dm = x_ref.shape
    rows = nb * CHUNK
    d_ff = wd_ref.shape[0]
    fb = FFN_BLOCK
    nblk = d_ff // fb

    @pl.when(t == 0)
    def _():
        cc_ref[...] = jnp.zeros(cc_ref.shape, _F32)

    @pl.when(t == nt)
    def _():
        cc_ref[0] = cc_in_ref[...]

    h_s[...] = _rms_rows(x_ref[...].reshape(rows, dm), nw_ref[...]).astype(_BF16)

    def up(i):
        h = h_s[...]
        cols = slice(i * fb, (i + 1) * fb)
        gcols = slice(d_ff + i * fb, d_ff + (i + 1) * fb)
        ext_s[i % 2, :, CONV_PAD:CONV_PAD + CHUNK, :] = _dot(h, wup_ref[:, cols]).reshape(nb, CHUNK, fb)
        g_s[i % 2] = _dot(h, wup_ref[:, gcols])

    def down(i):
        cols = slice(i * fb, (i + 1) * fb)
        ext = ext_s.at[i % 2]
        ext[:, CONV_PAD - 2:CONV_PAD, :] = cc_ref[0, :, :, cols]
        a3 = ext[:, CONV_PAD:CONV_PAD + CHUNK, :]
        cw = cw_ref[:, cols]
        conv = (cb_ref[:, cols] + ext[:, CONV_PAD - 2:CONV_PAD - 2 + CHUNK, :] * cw[0:1, :]
                + ext[:, CONV_PAD - 1:CONV_PAD - 1 + CHUNK, :] * cw[1:2, :] + a3 * cw[2:3, :])
        cc_ref[0, :, :, cols] = a3[:, CHUNK - 2:CHUNK, :]
        u = (_silu(conv).reshape(rows, fb) * g_s[i % 2]).astype(_BF16)
        return _dot(u, wd_ref[cols, :])

    up(0)
    y = None
    for i in range(nblk):
        if i + 1 < nblk:
            up(i + 1)
        part = down(i)
        prev = x_ref[...].reshape(rows, dm) if i == 0 else acc_s[...]
        if i + 1 < nblk:
            acc_s[...] = prev + part
        else:
            y = prev + part

    if last:
        yn = _rms_rows(y, nfw_ref[...]).reshape(nb, CHUNK, dm)

        @pl.when(t < nt)
        def _():
            yp_ref[...] = yn

        @pl.when(t == nt)
        def _():
            ys_ref[...] = yn
    else:
        y_ref[...] = y.reshape(y_ref.shape)


def _ffn_call(nt, last, lv, x, cc_in, norm_w, w_up, cw, cb, wd, nfw):
    nb, _, dm = x.shape
    rows = nb * CHUNK
    d_ff = wd.shape[1]
    x_spec = pl.BlockSpec((nb, CHUNK, dm), lambda t, l: (0, t, 0))
    in_specs = [
        x_spec,
        _layer_spec((None, nb, 2, d_ff)),
        _layer_spec((None, 1, dm)),
        _layer_spec((None, dm, 2 * d_ff)),
        _layer_spec((None, 3, d_ff)),
        _layer_spec((None, 1, d_ff)),
        _layer_spec((None, d_ff, dm)),
        _const_spec((1, dm)),
    ]
    cc_out = jax.ShapeDtypeStruct((2, nb, 2, d_ff), _F32)
    cc_spec = pl.BlockSpec((1, nb, 2, d_ff), lambda t, l: (t // nt, 0, 0, 0))
    if last:
        out_shape = [jax.ShapeDtypeStruct((nb, nt * CHUNK, dm), _F32), jax.ShapeDtypeStruct((nb, CHUNK, dm), _F32), cc_out]
        out_specs = [pl.BlockSpec((nb, CHUNK, dm), lambda t, l: (0, jnp.minimum(t, nt - 1), 0)),
                     pl.BlockSpec((nb, CHUNK, dm), lambda t, l: (0, 0, 0)), cc_spec]
    else:
        out_shape = [jax.ShapeDtypeStruct(x.shape, _F32), cc_out]
        out_specs = [x_spec, cc_spec]
    scratch = [
        pltpu.VMEM((rows, dm), _BF16),
        pltpu.VMEM((rows, dm), _F32),
        pltpu.VMEM((2, nb, CHUNK + CONV_PAD, FFN_BLOCK), _F32),
        pltpu.VMEM((2, rows, FFN_BLOCK), _F32),
    ]
    grid_spec = pltpu.PrefetchScalarGridSpec(
        num_scalar_prefetch=1, grid=(nt + 1,), in_specs=in_specs, out_specs=out_specs, scratch_shapes=scratch)
    return pl.pallas_call(
        functools.partial(_ffn_kernel, nt, last),
        grid_spec=grid_spec,
        out_shape=out_shape,
        input_output_aliases={} if last else {1: 0},
        compiler_params=pltpu.CompilerParams(dimension_semantics=("arbitrary",),
                                             vmem_limit_bytes=VMEM_LIMIT_BYTES),
        name="ffn_last" if last else "ffn",
    )(lv, x, cc_in, norm_w, w_up, cw, cb, wd, nfw)


def _constants():
    half = D_HEAD // 2
    inv = ROPE_BASE ** (-jnp.arange(half, dtype=_F32) / half)
    inv2 = jnp.concatenate([inv, inv])[None, :]
    r = np.arange(CHUNK)
    tril = (r[:, None] >= r[None, :]).astype(np.float32)
    tril3 = jnp.asarray(np.concatenate([tril, tril, tril], axis=1), _BF16)
    e8 = np.zeros((SUB * D_HEAD, D_HEAD), np.float32)
    cols = np.arange(CHUNK)
    for s in range(SUB):
        e8[s * D_HEAD:(s + 1) * D_HEAD, cols[cols % SUB == s]] = 1.0
    return inv2, tril3, jnp.asarray(e8, _BF16)


def kernel(x_prompt, x_sample, state_hgrn, state_ret, cache_conv, norm_mix_w, w_in, hg_lb_logits,
           hg_norm_w, ret_norm_w, ret_norm_b, w_branch, w_out, norm_ffn_w, w_up, conv_w, conv_b,
           w_down, norm_final_w):
    nb, tp, dm = x_prompt.shape
    depth = w_in.shape[0]
    d_ff = w_down.shape[1]
    assert x_sample.shape == (nb, CHUNK, dm) and tp % CHUNK == 0 and d_ff % FFN_BLOCK == 0
    nt = tp // CHUNK
    inv2, tril3, e8 = _constants()

    w_in_b = w_in.astype(_BF16)
    w_branch_b = w_branch.astype(_BF16)
    w_out_b = w_out.astype(_BF16)
    w_up_b = w_up.astype(_BF16)
    w_down_b = w_down.astype(_BF16)
    shg_in = jnp.swapaxes(state_hgrn, -1, -2)
    nmw = norm_mix_w[:, None, :]
    nfw_l = norm_ffn_w[:, None, :]
    hgw = hg_norm_w[:, None, :]
    rnw = ret_norm_w[:, None, :]
    rnb = ret_norm_b[:, None, :]
    cb = conv_b[:, None, :]
    nfin = norm_final_w[None, :]

    xs = (x_prompt, x_sample)
    hg_all, ret_all, cc_all = [], [], []
    for l in range(depth):
        lv = jnp.full((1,), l, jnp.int32)
        x, s_hg, s_ret = _mixer_call(nt, lv, xs, shg_in, state_ret, nmw, w_in_b, hg_lb_logits, hgw, rnw, rnb,
                                     w_branch_b, w_out_b, inv2, tril3, e8)
        *xs, s_cc = _ffn_call(nt, l == depth - 1, lv, x, cache_conv, nfw_l, w_up_b, conv_w, cb, w_down_b, nfin)
        hg_all.append(s_hg)
        ret_all.append(s_ret)
        cc_all.append(s_cc)
    y_prompt, y_sample = xs
    hg_all = jnp.swapaxes(jnp.stack(hg_all), -1, -2)
    ret_all = jnp.stack(ret_all)
    cc_all = jnp.stack(cc_all)
    return (y_prompt, y_sample, hg_all[:, 0], ret_all[:, 0], cc_all[:, 0],
            hg_all[:, 1], ret_all[:, 1], cc_all[:, 1])
```
